```python
import math
import jax, jax.numpy as jnp
from jax import lax
import numpy as np

D_MODEL = 1024
BATCH = 8
SEQ = 2048
DEPTH = 1

GDN_HEADS = 8
GDN_DK = 128
GDN_DV = 128
GDN_CONV = 4
GDN_CHUNK = 64
DIFF_HEADS = 8
DIFF_DH = 64
Q_BLOCK = 128
D_FF = 2816
FFN_CONV = 3
NORM_EPS = 1e-6

GDN_QK = GDN_HEADS * GDN_DK
GDN_V = GDN_HEADS * GDN_DV
DIFF_QK = DIFF_HEADS * 2 * DIFF_DH
DIFF_V = DIFF_HEADS * 2 * DIFF_DH

kernel_name = "hybrid_gdn_diffattn_convffn_block"


def _in_widths():
    return (GDN_QK, GDN_QK, GDN_V, GDN_V, GDN_HEADS, GDN_HEADS,
            DIFF_QK, DIFF_QK, DIFF_V, D_MODEL, D_MODEL)


def rmsnorm(x, g):
    xf = x.astype(jnp.float32)
    y = xf * lax.rsqrt(jnp.mean(xf * xf, axis=-1, keepdims=True) + NORM_EPS)
    return (y * g.astype(jnp.float32)).astype(x.dtype)


def l2norm(x):
    xf = x.astype(jnp.float32)
    return xf * lax.rsqrt(jnp.sum(xf * xf, axis=-1, keepdims=True) + NORM_EPS)


def causal_dwconv(x, w):
    K = w.shape[0]
    S = x.shape[1]
    xp = jnp.pad(x, ((0, 0), (K - 1, 0), (0, 0)))
    y = xp[:, 0:S, :] * w[0]
    for j in range(1, K):
        y = y + xp[:, j:j + S, :] * w[j]
    return y


def gated_delta_rule_chunked(q, k, v, g, beta):
    B, H, S, dk = q.shape
    dv = v.shape[-1]
    C = GDN_CHUNK
    NC = S // C
    q = q.reshape(B, H, NC, C, dk) * (dk ** -0.5)
    k = k.reshape(B, H, NC, C, dk)
    v = v.reshape(B, H, NC, C, dv)
    g = g.reshape(B, H, NC, C)
    beta = beta.reshape(B, H, NC, C)

    gc = jnp.cumsum(g, axis=-1)
    diff = gc[..., :, None] - gc[..., None, :]
    incl = jnp.tril(jnp.ones((C, C), dtype=bool))
    strict = jnp.tril(jnp.ones((C, C), dtype=bool), k=-1)
    L = jnp.where(incl, jnp.exp(jnp.where(incl, diff, 0.0)), 0.0)

    kb = k * beta[..., None]
    vb = v * beta[..., None]
    A = jnp.where(strict, jnp.einsum('bhnid,bhnjd->bhnij', kb, k) * L, 0.0)
    eye = jnp.eye(C, dtype=A.dtype)
    rhs = jnp.concatenate([kb * jnp.exp(gc)[..., None], vb], axis=-1)
    wu = lax.linalg.triangular_solve(eye + A, rhs, left_side=True, lower=True,
                                     unit_diagonal=True)
    w, u = wu[..., :dk], wu[..., dk:]

    Aqk = jnp.where(incl, jnp.einsum('bhnid,bhnjd->bhnij', q, k) * L, 0.0)
    q_dec = q * jnp.exp(gc)[..., None]
    k_dec = k * jnp.exp(gc[..., -1:] - gc)[..., None]
    g_last = jnp.exp(gc[..., -1])

    def step(state, inp):
        w_c, u_c, qd_c, kd_c, aqk_c, gl_c = inp
        v_new = u_c - jnp.einsum('bhcd,bhde->bhce', w_c, state)
        o_c = (jnp.einsum('bhcd,bhde->bhce', qd_c, state)
               + jnp.einsum('bhij,bhje->bhie', aqk_c, v_new))
        state = state * gl_c[..., None, None] + jnp.einsum('bhcd,bhce->bhde', kd_c, v_new)
        return state, o_c

    xs = tuple(jnp.moveaxis(t, 2, 0) for t in (w, u, q_dec, k_dec, Aqk, g_last))
    state0 = jnp.zeros((B, H, dk, dv), dtype=jnp.float32)
    _, o = lax.scan(step, state0, xs)
    return jnp.moveaxis(o, 0, 2).reshape(B, H, S, dv)


def diff_attention(q, k, v, lam):
    B, S, H, _, dh = q.shape
    NB = S // Q_BLOCK
    scale = dh ** -0.5
    qb = q.astype(jnp.float32).reshape(B, NB, Q_BLOCK, H, 2, dh)
    qb = jnp.transpose(qb, (1, 0, 3, 4, 2, 5))
    kt = jnp.transpose(k.astype(jnp.float32), (0, 2, 3, 1, 4))
    vt = jnp.transpose(v.astype(jnp.float32), (0, 2, 1, 3))
    kpos = jnp.arange(S)
    neg = jnp.finfo(jnp.float32).min

    def one_block(args):
        q_blk, bi = args
        s = jnp.einsum('bhtqd,bhtkd->bhtqk', q_blk, kt) * scale
        qpos = bi * Q_BLOCK + jnp.arange(Q_BLOCK)
        mask = kpos[None, :] <= qpos[:, None]
        p = jax.nn.softmax(jnp.where(mask, s, neg), axis=-1)
        a = p[:, :, 0] - lam * p[:, :, 1]
        return jnp.einsum('bhqk,bhke->bhqe', a, vt)

    out = lax.map(one_block, (qb, jnp.arange(NB)))
    out = jnp.transpose(out, (1, 0, 3, 2, 4))
    return out.reshape(B, S, H, 2 * dh)


def hybrid_mixer(h, w_in, conv_qkv_w, A_log, dt_bias, gdn_out_norm,
                 lambda_q1, lambda_k1, lambda_q2, lambda_k2, diff_subln,
                 w_branch_gdn, w_branch_diff, w_out, lambda_init):
    B, S, _ = h.shape
    proj = h @ w_in
    idx = np.cumsum(np.array(_in_widths()))[:-1]
    gq, gk, gv, gz, ga, gb, dq, dk, dv, gate_gdn, gate_diff = jnp.split(proj, idx, axis=-1)

    qkv = jax.nn.silu(causal_dwconv(jnp.concatenate([gq, gk, gv], axis=-1), conv_qkv_w))
    q, k, v = jnp.split(qkv, [GDN_QK, 2 * GDN_QK], axis=-1)
    q = jnp.transpose(l2norm(q.reshape(B, S, GDN_HEADS, GDN_DK)), (0, 2, 1, 3))
    k = jnp.transpose(l2norm(k.reshape(B, S, GDN_HEADS, GDN_DK)), (0, 2, 1, 3))
    v = jnp.transpose(v.reshape(B, S, GDN_HEADS, GDN_DV).astype(jnp.float32), (0, 2, 1, 3))
    g = -jnp.exp(A_log.astype(jnp.float32)) * jax.nn.softplus(
        ga.astype(jnp.float32) + dt_bias.astype(jnp.float32))
    beta = jax.nn.sigmoid(gb.astype(jnp.float32))
    o = gated_delta_rule_chunked(q, k, v, jnp.transpose(g, (0, 2, 1)),
                                 jnp.transpose(beta, (0, 2, 1)))
    o = jnp.transpose(o, (0, 2, 1, 3))
    o = rmsnorm(o, gdn_out_norm) * jax.nn.silu(
        gz.reshape(B, S, GDN_HEADS, GDN_DV).astype(jnp.float32))
    o_gdn = o.reshape(B, S, GDN_V).astype(h.dtype)

    lam = (jnp.exp(jnp.sum(lambda_q1.astype(jnp.float32) * lambda_k1.astype(jnp.float32)))
           - jnp.exp(jnp.sum(lambda_q2.astype(jnp.float32) * lambda_k2.astype(jnp.float32)))
           + lambda_init)
    a = diff_attention(dq.reshape(B, S, DIFF_HEADS, 2, DIFF_DH),
                       dk.reshape(B, S, DIFF_HEADS, 2, DIFF_DH),
                       dv.reshape(B, S, DIFF_HEADS, 2 * DIFF_DH), lam)
    a = rmsnorm(a, diff_subln) * (1.0 - lambda_init)
    o_diff = a.reshape(B, S, DIFF_V).astype(h.dtype)

    merged = (jax.nn.sigmoid(gate_gdn) * (o_gdn @ w_branch_gdn)
              + jax.nn.sigmoid(gate_diff) * (o_diff @ w_branch_diff))
    return merged @ w_out


def conv_ffn(h, w_up, conv_w, conv_b, w_down):
    u = causal_dwconv(h @ w_up, conv_w) + conv_b
    gate, up = jnp.split(u, 2, axis=-1)
    return (jax.nn.silu(gate) * up) @ w_down


def setup_inputs(seed: int = 0) -> dict:
    key = jax.random.key(seed)
    ks = jax.random.split(key, 24)
    f32 = jnp.float32
    d_in = sum(_in_widths())

    def nrm(k, shape, scale):
        return jax.random.normal(k, shape, f32) * scale

    def gain(k, n):
        return 1.0 + 0.05 * jax.random.normal(k, (DEPTH, n), f32)

    dt = jnp.exp(jax.random.uniform(ks[5], (DEPTH, GDN_HEADS), f32,
                                    math.log(1e-3), math.log(1e-1)))
    return {
        "x": nrm(ks[0], (BATCH, SEQ, D_MODEL), 1.0),
        "norm_mix_pre": gain(ks[1], D_MODEL),
        "w_in": nrm(ks[2], (DEPTH, D_MODEL, d_in), D_MODEL ** -0.5),
        "conv_qkv_w": nrm(ks[3], (DEPTH, GDN_CONV, 2 * GDN_QK + GDN_V), GDN_CONV ** -0.5),
        "gdn_A_log": jnp.log(jax.random.uniform(ks[4], (DEPTH, GDN_HEADS), f32, 1.0, 16.0)),
        "gdn_dt_bias": dt + jnp.log(-jnp.expm1(-dt)),
        "gdn_out_norm": gain(ks[6], GDN_DV),
        "lambda_q1": nrm(ks[7], (DEPTH, DIFF_DH), 0.1),
        "lambda_k1": nrm(ks[8], (DEPTH, DIFF_DH), 0.1),
        "lambda_q2": nrm(ks[9], (DEPTH, DIFF_DH), 0.1),
        "lambda_k2": nrm(ks[10], (DEPTH, DIFF_DH), 0.1),
        "diff_subln": gain(ks[11], 2 * DIFF_DH),
        "w_branch_gdn": nrm(ks[12], (DEPTH, GDN_V, D_MODEL), GDN_V ** -0.5),
        "w_branch_diff": nrm(ks[13], (DEPTH, DIFF_V, D_MODEL), DIFF_V ** -0.5),
        "w_out": nrm(ks[14], (DEPTH, D_MODEL, D_MODEL), D_MODEL ** -0.5),
        "norm_mix_post": gain(ks[15], D_MODEL),
        "norm_ffn_pre": gain(ks[16], D_MODEL),
        "w_up": nrm(ks[17], (DEPTH, D_MODEL, 2 * D_FF), D_MODEL ** -0.5),
        "ffn_conv_w": nrm(ks[18], (DEPTH, FFN_CONV, 2 * D_FF), FFN_CONV ** -0.5),
        "ffn_conv_b": nrm(ks[19], (DEPTH, 2 * D_FF), 0.02),
        "w_down": nrm(ks[20], (DEPTH, D_FF, D_MODEL), D_FF ** -0.5),
        "norm_ffn_post": gain(ks[21], D_MODEL),
    }


def reference(x, norm_mix_pre, w_in, conv_qkv_w, gdn_A_log, gdn_dt_bias, gdn_out_norm,
              lambda_q1, lambda_k1, lambda_q2, lambda_k2, diff_subln,
              w_branch_gdn, w_branch_diff, w_out, norm_mix_post,
              norm_ffn_pre, w_up, ffn_conv_w, ffn_conv_b, w_down, norm_ffn_post):
    for l in range(DEPTH):
        lambda_init = 0.8 - 0.6 * math.exp(-0.3 * l)
        h = rmsnorm(x, norm_mix_pre[l])
        m = hybrid_mixer(h, w_in[l], conv_qkv_w[l], gdn_A_log[l], gdn_dt_bias[l],
                         gdn_out_norm[l], lambda_q1[l], lambda_k1[l], lambda_q2[l],
                         lambda_k2[l], diff_subln[l], w_branch_gdn[l], w_branch_diff[l],
                         w_out[l], lambda_init)
        x = x + rmsnorm(m, norm_mix_post[l])
        h = rmsnorm(x, norm_ffn_pre[l])
        f = conv_ffn(h, w_up[l], ffn_conv_w[l], ffn_conv_b[l], w_down[l])
        x = x + rmsnorm(f, norm_ffn_post[l])
    return x
```

```python
import functools
import math

import jax
import jax.numpy as jnp
from jax import lax
from jax.experimental import pallas as pl
from jax.experimental.pallas import tpu as pltpu

F32 = jnp.float32
BF16 = jnp.bfloat16

D_MODEL = 1024
GDN_HEADS = 8
GDN_DK = 128
GDN_DV = 128
GDN_CONV = 4
DIFF_HEADS = 8
DIFF_DH = 64
D_FF = 2816
FFN_CONV = 3
NORM_EPS = 1e-6

GDN_QK = GDN_HEADS * GDN_DK
GDN_V = GDN_HEADS * GDN_DV
DIFF_QK = DIFF_HEADS * 2 * DIFF_DH
DIFF_V = DIFF_HEADS * 2 * DIFF_DH

LANES = 128
GDN_CHUNK = 128
VMEM_LIMIT = 56 * 1024 * 1024


def _dot(a, b):
    return jnp.dot(a, b, preferred_element_type=F32)


def _dot_nt(a, b):
    return lax.dot_general(a, b, (((1,), (1,)), ((), ())), preferred_element_type=F32)


def _bdot(a, b):
    return lax.dot_general(a, b, (((2,), (1,)), ((0,), (0,))), preferred_element_type=F32)


def _bdot_nt(a, b):
    return lax.dot_general(a, b, (((2,), (2,)), ((0,), (0,))), preferred_element_type=F32)


def _sigmoid(x):
    return 1.0 / (1.0 + jnp.exp(-x))


def _silu(x):
    return x * _sigmoid(x)


def _const_spec(shape):
    nd = len(shape)
    return pl.BlockSpec(shape, lambda *_: (0,) * nd, pipeline_mode=pl.Buffered(1))


_IN_CHUNK = 512


def _inproj_kernel(x_ref, g_ref, w_ref, gdn_ref, ab_ref, dqkv_ref, gates_ref, h_scr):
    x = x_ref[...]
    ms = jnp.mean(x * x, axis=-1, keepdims=True)
    h_scr[...] = (x * lax.rsqrt(ms + NORM_EPS) * g_ref[...]).astype(BF16)

    def emit(out_ref, w_off, width):
        for c0 in range(0, width, _IN_CHUNK):
            cw = min(_IN_CHUNK, width - c0)
            acc = _dot(h_scr[...], w_ref[:, w_off + c0:w_off + c0 + cw])
            out_ref[:, c0:c0 + cw] = acc.astype(out_ref.dtype)

    emit(gdn_ref, 0, 4 * GDN_QK)
    emit(ab_ref, 4 * GDN_QK, LANES)
    emit(dqkv_ref, 4 * GDN_QK + LANES, 3 * DIFF_QK)
    emit(gates_ref, 4 * GDN_QK + LANES + 3 * DIFF_QK, 2 * D_MODEL)


def _inproj(x2, g, w_all, tm):
    T = x2.shape[0]
    n_all = w_all.shape[1]
    return pl.pallas_call(
        _inproj_kernel,
        grid=(T // tm,),
        in_specs=[
            pl.BlockSpec((tm, D_MODEL), lambda i: (i, 0)),
            _const_spec((1, D_MODEL)),
            _const_spec((D_MODEL, n_all)),
        ],
        out_specs=[
            pl.BlockSpec((tm, 4 * GDN_QK), lambda i: (i, 0)),
            pl.BlockSpec((tm, LANES), lambda i: (i, 0)),
            pl.BlockSpec((tm, 3 * DIFF_QK), lambda i: (i, 0)),
            pl.BlockSpec((tm, 2 * D_MODEL), lambda i: (i, 0)),
        ],
        out_shape=[
            jax.ShapeDtypeStruct((T, 4 * GDN_QK), BF16),
            jax.ShapeDtypeStruct((T, LANES), F32),
            jax.ShapeDtypeStruct((T, 3 * DIFF_QK), BF16),
            jax.ShapeDtypeStruct((T, 2 * D_MODEL), BF16),
        ],
        scratch_shapes=[pltpu.VMEM((tm, D_MODEL), BF16)],
        compiler_params=pltpu.CompilerParams(
            dimension_semantics=("arbitrary",), vmem_limit_bytes=VMEM_LIMIT),
        name="inproj",
    )(x2, g, w_all)


def _split3_dot(x, sel):
    hi = x.astype(BF16)
    r1 = x - hi.astype(F32)
    mid = r1.astype(BF16)
    lo = (r1 - mid.astype(F32)).astype(BF16)
    return _dot(hi, sel) + _dot(mid, sel) + _dot(lo, sel)


def _gdn_kernel(q_ref, k_ref, v_ref, z_ref, ab_ref, cwq_ref, cwk_ref, cwv_ref, gp_ref, onorm_ref,
                o_ref, wq_s, u_s, aqk_s, kdt_s, gl_s, o_s):
    head = pl.program_id(1)
    S = q_ref.shape[1]
    C = GDN_CHUNK
    NC = S // C
    row = lax.broadcasted_iota(jnp.int32, (S, LANES), 0)

    def conv_silu(x_ref, w_ref):
        x = x_ref[0].astype(F32)
        w = w_ref[...]
        y = x * w[GDN_CONV - 1:GDN_CONV, :]
        for sh in range(1, GDN_CONV):
            xs = jnp.where(row >= sh, pltpu.roll(x, sh, 0), 0.0)
            y = y + xs * w[GDN_CONV - 1 - sh:GDN_CONV - sh, :]
        return _silu(y)

    def l2norm(x):
        return x * lax.rsqrt(jnp.sum(x * x, axis=-1, keepdims=True) + NORM_EPS)

    q = l2norm(conv_silu(q_ref, cwq_ref)) * (GDN_DK ** -0.5)
    k = l2norm(conv_silu(k_ref, cwk_ref))
    v = conv_silu(v_ref, cwv_ref)

    ri = lax.broadcasted_iota(jnp.int32, (LANES, LANES), 0)
    sel_a = (ri == head).astype(BF16)
    sel_b = (ri == head + GDN_HEADS).astype(BF16)
    ab = ab_ref[0]
    a_bc = _split3_dot(ab, sel_a)
    b_bc = _split3_dot(ab, sel_b)
    gp = _split3_dot(gp_ref[...], sel_a)
    a_log = gp[0:1, :]
    dt_bias = gp[1:2, :]
    xs = a_bc + dt_bias
    softplus = jnp.maximum(xs, 0.0) + jnp.log1p(jnp.exp(-jnp.abs(xs)))
    g = -jnp.exp(a_log) * softplus
    beta = _sigmoid(b_bc)

    rin = jnp.bitwise_and(row, C - 1)
    gc = g
    sh = 1
    while sh < C:
        gc = gc + jnp.where(rin >= sh, pltpu.roll(gc, sh, 0), 0.0)
        sh *= 2

    gc3 = gc.reshape(NC, C, LANES)
    glast = gc3[:, C - 1:C, :]
    eg = jnp.exp(gc)
    kb = k * beta
    rhs_w = (kb * eg).reshape(NC, C, GDN_DK)
    rhs_u = (v * beta).reshape(NC, C, GDN_DV)
    q_dec = (q * eg).reshape(NC, C, GDN_DK)
    k3 = k.reshape(NC, C, GDN_DK)
    k_dec = k3 * jnp.exp(glast - gc3)

    k3b = k3.astype(BF16)
    kk = _bdot_nt(kb.reshape(NC, C, GDN_DK).astype(BF16), k3b)
    qk = _bdot_nt(q.reshape(NC, C, GDN_DK).astype(BF16), k3b)

    ii = lax.broadcasted_iota(jnp.int32, (C, C), 0)
    jj = lax.broadcasted_iota(jnp.int32, (C, C), 1)
    decay = jnp.exp(jnp.minimum(gc3 - jnp.swapaxes(gc3, 1, 2), 0.0))
    a_mat = jnp.where(ii > jj, kk * decay, 0.0)
    aqk = jnp.where(ii >= jj, qk * decay, 0.0)

    eye = (ii == jj).astype(F32)
    p = eye - jnp.where((ii >> 1) == (jj >> 1), a_mat, 0.0)
    lvl = 1
    while (1 << lvl) < C:
        in_parent = (ii >> (lvl + 1)) == (jj >> (lvl + 1))
        in_child = (ii >> lvl) == (jj >> lvl)
        e = jnp.where(jnp.logical_and(in_parent, jnp.logical_not(in_child)), a_mat, 0.0)
        pb = p.astype(BF16)
        pe = _bdot(pb, e.astype(BF16))
        p = p - _bdot(pe.astype(BF16), pb)
        lvl += 1

    rhs = jnp.concatenate([rhs_w, rhs_u], axis=-1).astype(BF16)
    wu = _bdot(p.astype(BF16), rhs)
    wq_s[:, 0:C, :] = wu[:, :, 0:GDN_DK].astype(BF16)
    wq_s[:, C:2 * C, :] = q_dec.astype(BF16)
    u_s[...] = wu[:, :, GDN_DK:GDN_DK + GDN_DV]
    aqk_s[...] = aqk.astype(BF16)
    kdt_s[...] = jnp.swapaxes(k_dec, 1, 2).astype(BF16)
    gl_s[...] = jnp.broadcast_to(jnp.exp(glast), (NC, 8, LANES))

    def chunk_step(c, state):
        sb = state.astype(BF16)
        ws = _dot(wq_s[c], sb)
        v_new = u_s[c] - ws[0:C, :]
        vb = v_new.astype(BF16)
        o_s[c] = ws[C:2 * C, :] + _dot(aqk_s[c], vb)
        return state * gl_s[c][0:1, :] + _dot(kdt_s[c], vb)

    lax.fori_loop(0, NC, chunk_step, jnp.zeros((GDN_DK, GDN_DV), F32))

    o = o_s[...].reshape(S, GDN_DV)
    ms = jnp.mean(o * o, axis=-1, keepdims=True)
    o = o * lax.rsqrt(ms + NORM_EPS) * onorm_ref[...]
    o_ref[0] = (o * _silu(z_ref[0].astype(F32))).astype(o_ref.dtype)


def _gdn(gdn3, ab3, conv_w, gparams, onorm):
    B, S, _ = gdn3.shape
    NC = S // GDN_CHUNK
    H = GDN_HEADS
    seq_spec = lambda off: pl.BlockSpec((1, S, LANES), lambda b, h, off=off: (b, 0, off + h))
    cw_spec = lambda off: pl.BlockSpec((GDN_CONV, LANES), lambda b, h, off=off: (0, off + h))
    return pl.pallas_call(
        _gdn_kernel,
        grid=(B, H),
        in_specs=[
            seq_spec(0), seq_spec(H), seq_spec(2 * H), seq_spec(3 * H),
            pl.BlockSpec((1, S, LANES), lambda b, h: (b, 0, 0)),
            cw_spec(0), cw_spec(H), cw_spec(2 * H),
            pl.BlockSpec((8, LANES), lambda b, h: (0, 0)),
            pl.BlockSpec((1, LANES), lambda b, h: (0, 0)),
        ],
        out_specs=pl.BlockSpec((1, S, LANES), lambda b, h: (b, 0, h)),
        out_shape=jax.ShapeDtypeStruct((B, S, GDN_V), BF16),
        scratch_shapes=[
            pltpu.VMEM((NC, 2 * GDN_CHUNK, GDN_DK), BF16),
            pltpu.VMEM((NC, GDN_CHUNK, GDN_DV), F32),
            pltpu.VMEM((NC, GDN_CHUNK, GDN_CHUNK), BF16),
            pltpu.VMEM((NC, GDN_DK, GDN_CHUNK), BF16),
            pltpu.VMEM((NC, 8, LANES), F32),
            pltpu.VMEM((NC, GDN_CHUNK, GDN_DV), F32),
        ],
        compiler_params=pltpu.CompilerParams(
            dimension_semantics=("arbitrary", "arbitrary"), vmem_limit_bytes=VMEM_LIMIT),
        name="gdn",
    )(gdn3, gdn3, gdn3, gdn3, ab3, conv_w, conv_w, conv_w, gparams, onorm)


_ATT_BLOCK = 256


def _diffattn_kernel(lam_ref, subln_ref, q_ref, k_ref, v_ref, o_ref, *, lambda_init):
    qi = pl.program_id(2)
    tq = q_ref.shape[1]
    tk = tq
    lane = lax.broadcasted_iota(jnp.int32, (tq, 2 * DIFF_DH), 1)
    qs = q_ref[0] * (DIFF_DH ** -0.5)
    zero = jnp.zeros_like(qs)
    q1 = jnp.where(lane < DIFF_DH, qs, zero)
    q2 = jnp.where(lane >= DIFF_DH, qs, zero)

    lp = lam_ref[...]
    lam = (jnp.exp(jnp.sum(lp[0:1, :] * lp[1:2, :], axis=-1, keepdims=True))
           - jnp.exp(jnp.sum(lp[2:3, :] * lp[3:4, :], axis=-1, keepdims=True))
           + lambda_init)

    def update(carry, qm, kblk, vblk, mask):
        m, l, acc = carry
        s = _dot_nt(qm, kblk)
        if mask is not None:
            s = jnp.where(mask, s, jnp.finfo(F32).min)
        m_new = jnp.maximum(m, jnp.max(s, axis=-1, keepdims=True))
        alpha = jnp.exp(m - m_new)
        p = jnp.exp(s - m_new)
        l_new = alpha * l + jnp.sum(p, axis=-1, keepdims=True)
        acc_new = alpha * acc + _dot(p.astype(BF16), vblk)
        return m_new, l_new, acc_new

    def init():
        return (jnp.full((tq, 1), -jnp.inf, F32), jnp.zeros((tq, 1), F32),
                jnp.zeros((tq, 2 * DIFF_DH), F32))

    def body(j, carry):
        c1, c2 = carry
        start = pl.multiple_of(j * tk, tk)
        kblk = k_ref[0, pl.ds(start, tk), :]
        vblk = v_ref[0, pl.ds(start, tk), :]
        return update(c1, q1, kblk, vblk, None), update(c2, q2, kblk, vblk, None)

    c1, c2 = lax.fori_loop(0, qi, body, (init(), init()))

    start = pl.multiple_of(qi * tk, tk)
    kblk = k_ref[0, pl.ds(start, tk), :]
    vblk = v_ref[0, pl.ds(start, tk), :]
    causal = (lax.broadcasted_iota(jnp.int32, (tq, tk), 1)
              <= lax.broadcasted_iota(jnp.int32, (tq, tk), 0))
    _, l1, acc1 = update(c1, q1, kblk, vblk, causal)
    _, l2, acc2 = update(c2, q2, kblk, vblk, causal)

    a = acc1 / l1 - lam * (acc2 / l2)
    ms = jnp.mean(a * a, axis=-1, keepdims=True)
    a = a * lax.rsqrt(ms + NORM_EPS) * subln_ref[...] * (1.0 - lambda_init)
    o_ref[0] = a.astype(o_ref.dtype)


def _diffattn(dqkv3, lam_params, subln, lambda_init):
    B, S, _ = dqkv3.shape
    H = DIFF_HEADS
    tq = min(_ATT_BLOCK, S)
    return pl.pallas_call(
        functools.partial(_diffattn_kernel, lambda_init=lambda_init),
        grid=(B, H, S // tq),
        in_specs=[
            pl.BlockSpec((8, DIFF_DH), lambda b, h, i: (0, 0)),
            pl.BlockSpec((1, 2 * DIFF_DH), lambda b, h, i: (0, 0)),
            pl.BlockSpec((1, tq, 2 * DIFF_DH), lambda b, h, i: (b, i, h)),
            pl.BlockSpec((1, S, 2 * DIFF_DH), lambda b, h, i: (b, 0, H + h)),
            pl.BlockSpec((1, S, 2 * DIFF_DH), lambda b, h, i: (b, 0, 2 * H + h)),
        ],
        out_specs=pl.BlockSpec((1, tq, 2 * DIFF_DH), lambda b, h, i: (b, i, h)),
        out_shape=jax.ShapeDtypeStruct((B, S, DIFF_V), BF16),
        compiler_params=pltpu.CompilerParams(
            dimension_semantics=("arbitrary", "arbitrary", "arbitrary"),
            vmem_limit_bytes=VMEM_LIMIT),
        name="diffattn",
    )(lam_params, subln, dqkv3, dqkv3, dqkv3)


def _merge_kernel(x_ref, og_ref, od_ref, gates_ref, wbg_ref, wbd_ref, wout_ref,
                  npost_ref, npre_ref, x1_ref, h2_ref):
    mg = _dot(og_ref[...], wbg_ref[...])
    md = _dot(od_ref[...], wbd_ref[...])
    gg = _sigmoid(gates_ref[:, 0:D_MODEL].astype(F32))
    gd = _sigmoid(gates_ref[:, D_MODEL:2 * D_MODEL].astype(F32))
    merged = (gg * mg + gd * md).astype(BF16)
    m = _dot(merged, wout_ref[...])
    ms = jnp.mean(m * m, axis=-1, keepdims=True)
    x1 = x_ref[...] + m * lax.rsqrt(ms + NORM_EPS) * npost_ref[...]
    x1_ref[...] = x1
    ms1 = jnp.mean(x1 * x1, axis=-1, keepdims=True)
    h2_ref[...] = (x1 * lax.rsqrt(ms1 + NORM_EPS) * npre_ref[...]).astype(BF16)


def _merge(x2, og, od, gates, wbg, wbd, wout, npost, npre, tm):
    T = x2.shape[0]
    row = lambda w: pl.BlockSpec((tm, w), lambda i: (i, 0))
    return pl.pallas_call(
        _merge_kernel,
        grid=(T // tm,),
        in_specs=[
            row(D_MODEL), row(GDN_V), row(DIFF_V), row(2 * D_MODEL),
            _const_spec((GDN_V, D_MODEL)), _const_spec((DIFF_V, D_MODEL)),
            _const_spec((D_MODEL, D_MODEL)),
            _const_spec((1, D_MODEL)), _const_spec((1, D_MODEL)),
        ],
        out_specs=[row(D_MODEL), row(D_MODEL)],
        out_shape=[jax.ShapeDtypeStruct((T, D_MODEL), F32),
                   jax.ShapeDtypeStruct((T, D_MODEL), BF16)],
        compiler_params=pltpu.CompilerParams(
            dimension_semantics=("arbitrary",), vmem_limit_bytes=VMEM_LIMIT),
        name="merge",
    )(x2, og, od, gates, wbg, wbd, wout, npost, npre)


_FFN_CHUNK = 256
_HALO = 8


def _ffn_kernel(x1_ref, h2_ref, wup_ref, cw_ref, cb_ref, wdown_ref, npost_ref, out_ref,
                ubuf, carry, facc):
    si = pl.program_id(1)
    tm = h2_ref.shape[1]
    n_chunks = D_FF // _FFN_CHUNK

    @pl.when(si == 0)
    def _():
        carry[...] = jnp.zeros_like(carry)

    h2 = h2_ref[0]

    def conv_half(col0):
        u = _dot(h2, wup_ref[:, col0:col0 + _FFN_CHUNK])
        ubuf[0:_HALO, :] = carry[:, col0:col0 + _FFN_CHUNK]
        ubuf[_HALO:_HALO + tm, :] = u
        carry[:, col0:col0 + _FFN_CHUNK] = ubuf[tm:tm + _HALO, :]
        w = cw_ref[:, col0:col0 + _FFN_CHUNK]
        y = u * w[FFN_CONV - 1:FFN_CONV, :]
        for sh in range(1, FFN_CONV):
            y = y + ubuf[_HALO - sh:_HALO - sh + tm, :] * w[FFN_CONV - 1 - sh:FFN_CONV - sh, :]
        return y + cb_ref[:, col0:col0 + _FFN_CHUNK]

    for c in range(n_chunks):
        gate = conv_half(c * _FFN_CHUNK)
        up = conv_half(D_FF + c * _FFN_CHUNK)
        act = (_silu(gate) * up).astype(BF16)
        part = _dot(act, wdown_ref[c * _FFN_CHUNK:(c + 1) * _FFN_CHUNK, :])
        if c == 0:
            facc[...] = part
        else:
            facc[...] += part

    f = facc[...]
    ms = jnp.mean(f * f, axis=-1, keepdims=True)
    out_ref[0] = x1_ref[0] + f * lax.rsqrt(ms + NORM_EPS) * npost_ref[...]


def _ffn(x1_3, h2_3, wup, cw, cb, wdown, npost, tm):
    B, S, _ = x1_3.shape
    seq = lambda: pl.BlockSpec((1, tm, D_MODEL), lambda b, s: (b, s, 0))
    return pl.pallas_call(
        _ffn_kernel,
        grid=(B, S // tm),
        in_specs=[
            seq(), seq(),
            _const_spec((D_MODEL, 2 * D_FF)),
            _const_spec((FFN_CONV, 2 * D_FF)),
            _const_spec((1, 2 * D_FF)),
            _const_spec((D_FF, D_MODEL)),
            _const_spec((1, D_MODEL)),
        ],
        out_specs=seq(),
        out_shape=jax.ShapeDtypeStruct((B, S, D_MODEL), F32),
        scratch_shapes=[
            pltpu.VMEM((tm + _HALO, _FFN_CHUNK), F32),
            pltpu.VMEM((_HALO, 2 * D_FF), F32),
            pltpu.VMEM((tm, D_MODEL), F32),
        ],
        compiler_params=pltpu.CompilerParams(
            dimension_semantics=("arbitrary", "arbitrary"), vmem_limit_bytes=VMEM_LIMIT),
        name="convffn",
    )(x1_3, h2_3, wup, cw, cb, wdown, npost)


def _row_tile(n, want):
    t = min(want, n)
    while n % t:
        t //= 2
    return t


def _layer(x, lambda_init, norm_mix_pre, w_in, conv_qkv_w, gdn_A_log, gdn_dt_bias, gdn_out_norm,
           lambda_q1, lambda_k1, lambda_q2, lambda_k2, diff_subln, w_branch_gdn, w_branch_diff,
           w_out, norm_mix_post, norm_ffn_pre, w_up, ffn_conv_w, ffn_conv_b, w_down, norm_ffn_post):
    B, S, D = x.shape
    T = B * S
    x2 = x.reshape(T, D)
    row2 = lambda v: v.reshape(1, -1).astype(F32)

    n_gdn = 4 * GDN_QK
    w_ab = jnp.pad(w_in[:, n_gdn:n_gdn + 2 * GDN_HEADS], ((0, 0), (0, LANES - 2 * GDN_HEADS)))
    w_all = jnp.concatenate([w_in[:, :n_gdn], w_ab, w_in[:, n_gdn + 2 * GDN_HEADS:]],
                            axis=1).astype(BF16)

    gdn, ab, dqkv, gates = _inproj(x2, row2(norm_mix_pre), w_all, _row_tile(T, 512))

    gparams = jnp.zeros((8, LANES), F32)
    gparams = gparams.at[0, :GDN_HEADS].set(gdn_A_log.astype(F32))
    gparams = gparams.at[1, :GDN_HEADS].set(gdn_dt_bias.astype(F32))
    o_gdn = _gdn(gdn.reshape(B, S, -1), ab.reshape(B, S, LANES), conv_qkv_w.astype(F32),
                 gparams, row2(gdn_out_norm))

    lam_params = jnp.zeros((8, DIFF_DH), F32)
    lam_params = lam_params.at[0].set(lambda_q1.astype(F32)).at[1].set(lambda_k1.astype(F32))
    lam_params = lam_params.at[2].set(lambda_q2.astype(F32)).at[3].set(lambda_k2.astype(F32))
    o_diff = _diffattn(dqkv.reshape(B, S, -1), lam_params, row2(diff_subln), lambda_init)

    x1, h2 = _merge(x2, o_gdn.reshape(T, -1), o_diff.reshape(T, -1), gates,
                    w_branch_gdn.astype(BF16), w_branch_diff.astype(BF16), w_out.astype(BF16),
                    row2(norm_mix_post), row2(norm_ffn_pre), _row_tile(T, 512))

    out = _ffn(x1.reshape(B, S, D), h2.reshape(B, S, D), w_up.astype(BF16),
               ffn_conv_w.astype(F32), row2(ffn_conv_b), w_down.astype(BF16),
               row2(norm_ffn_post), _row_tile(S, 512))
    return out


def kernel(x, norm_mix_pre, w_in, conv_qkv_w, gdn_A_log, gdn_dt_bias, gdn_out_norm, lambda_q1, lambda_k1, lambda_q2, lambda_k2, diff_subln, w_branch_gdn, w_branch_diff, w_out, norm_mix_post, norm_ffn_pre, w_up, ffn_conv_w, ffn_conv_b, w_down, norm_ffn_post):
    depth = w_in.shape[0]
    for l in range(depth):
        lambda_init = 0.8 - 0.6 * math.exp(-0.3 * l)
        x = _layer(x, lambda_init, norm_mix_pre[l], w_in[l], conv_qkv_w[l], gdn_A_log[l],
                   gdn_dt_bias[l], gdn_out_norm[l], lambda_q1[l], lambda_k1[l], lambda_q2[l],
                   lambda_k2[l], diff_subln[l], w_branch_gdn[l], w_branch_diff[l], w_out[l],
                   norm_mix_post[l], norm_ffn_pre[l], w_up[l], ffn_conv_w[l], ffn_conv_b[l],
                   w_down[l], norm_ffn_post[l])
    return x
```

```python
import functools
import math

import jax
import jax.numpy as jnp
from jax import lax
from jax.experimental import pallas as pl
from jax.experimental.pallas import tpu as pltpu

F32 = jnp.float32
BF16 = jnp.bfloat16

D_MODEL = 1024
GDN_HEADS = 8
GDN_DK = 128
GDN_DV = 128
GDN_CONV = 4
DIFF_HEADS = 8
DIFF_DH = 64
D_FF = 2816
FFN_CONV = 3
NORM_EPS = 1e-6

GDN_QK = GDN_HEADS * GDN_DK
GDN_V = GDN_HEADS * GDN_DV
DIFF_QK = DIFF_HEADS * 2 * DIFF_DH
DIFF_V = DIFF_HEADS * 2 * DIFF_DH

LANES = 128
GDN_CHUNK = 128
VMEM_LIMIT = 56 * 1024 * 1024


def _dot(a, b):
    return jnp.dot(a, b, preferred_element_type=F32)


def _dot_nt(a, b):
    return lax.dot_general(a, b, (((1,), (1,)), ((), ())), preferred_element_type=F32)


def _bdot(a, b):
    return lax.dot_general(a, b, (((2,), (1,)), ((0,), (0,))), preferred_element_type=F32)


def _bdot_nt(a, b):
    return lax.dot_general(a, b, (((2,), (2,)), ((0,), (0,))), preferred_element_type=F32)


def _sigmoid(x):
    return 1.0 / (1.0 + jnp.exp(-x))


def _silu(x):
    return x * _sigmoid(x)


def _const_spec(shape):
    nd = len(shape)
    return pl.BlockSpec(shape, lambda *_: (0,) * nd, pipeline_mode=pl.Buffered(1))


_IN_CHUNK = 512


def _inproj_kernel(x_ref, g_ref, w_ref, gdn_ref, ab_ref, dqkv_ref, gates_ref, h_scr):
    x = x_ref[...]
    ms = jnp.mean(x * x, axis=-1, keepdims=True)
    h_scr[...] = (x * lax.rsqrt(ms + NORM_EPS) * g_ref[...]).astype(BF16)

    def emit(out_ref, w_off, width):
        for c0 in range(0, width, _IN_CHUNK):
            cw = min(_IN_CHUNK, width - c0)
            acc = _dot(h_scr[...], w_ref[:, w_off + c0:w_off + c0 + cw])
            out_ref[:, c0:c0 + cw] = acc.astype(out_ref.dtype)

    emit(gdn_ref, 0, 4 * GDN_QK)
    emit(ab_ref, 4 * GDN_QK, LANES)
    emit(dqkv_ref, 4 * GDN_QK + LANES, 3 * DIFF_QK)
    emit(gates_ref, 4 * GDN_QK + LANES + 3 * DIFF_QK, 2 * D_MODEL)


def _inproj(x2, g, w_all, tm):
    T = x2.shape[0]
    n_all = w_all.shape[1]
    return pl.pallas_call(
        _inproj_kernel,
        grid=(T // tm,),
        in_specs=[
            pl.BlockSpec((tm, D_MODEL), lambda i: (i, 0)),
            _const_spec((1, D_MODEL)),
            _const_spec((D_MODEL, n_all)),
        ],
        out_specs=[
            pl.BlockSpec((tm, 4 * GDN_QK), lambda i: (i, 0)),
            pl.BlockSpec((tm, LANES), lambda i: (i, 0)),
            pl.BlockSpec((tm, 3 * DIFF_QK), lambda i: (i, 0)),
            pl.BlockSpec((tm, 2 * D_MODEL), lambda i: (i, 0)),
        ],
        out_shape=[
            jax.ShapeDtypeStruct((T, 4 * GDN_QK), BF16),
            jax.ShapeDtypeStruct((T, LANES), F32),
            jax.ShapeDtypeStruct((T, 3 * DIFF_QK), BF16),
            jax.ShapeDtypeStruct((T, 2 * D_MODEL), BF16),
        ],
        scratch_shapes=[pltpu.VMEM((tm, D_MODEL), BF16)],
        compiler_params=pltpu.CompilerParams(
            dimension_semantics=("arbitrary",), vmem_limit_bytes=VMEM_LIMIT),
        name="inproj",
    )(x2, g, w_all)


def _split3_dot(x, sel):
    hi = x.astype(BF16)
    r1 = x - hi.astype(F32)
    mid = r1.astype(BF16)
    lo = (r1 - mid.astype(F32)).astype(BF16)
    return _dot(hi, sel) + _dot(mid, sel) + _dot(lo, sel)


def _gdn_kernel(q_ref, k_ref, v_ref, z_ref, ab_ref, cwq_ref, cwk_ref, cwv_ref, gp_ref, onorm_ref,
                o_ref, wq_s, u_s, aqk_s, kdt_s, gl_s, o_s):
    head = pl.program_id(1)
    S = q_ref.shape[1]
    C = GDN_CHUNK
    NC = S // C
    row = lax.broadcasted_iota(jnp.int32, (S, LANES), 0)

    def conv_silu(x_ref, w_ref):
        x = x_ref[0].astype(F32)
        w = w_ref[...]
        y = x * w[GDN_CONV - 1:GDN_CONV, :]
        for sh in range(1, GDN_CONV):
            xs = jnp.where(row >= sh, pltpu.roll(x, sh, 0), 0.0)
            y = y + xs * w[GDN_CONV - 1 - sh:GDN_CONV - sh, :]
        return _silu(y)

    def l2norm(x):
        return x * lax.rsqrt(jnp.sum(x * x, axis=-1, keepdims=True) + NORM_EPS)

    q = l2norm(conv_silu(q_ref, cwq_ref)) * (GDN_DK ** -0.5)
    k = l2norm(conv_silu(k_ref, cwk_ref))
    v = conv_silu(v_ref, cwv_ref)

    ri = lax.broadcasted_iota(jnp.int32, (LANES, LANES), 0)
    sel_a = (ri == head).astype(BF16)
    sel_b = (ri == head + GDN_HEADS).astype(BF16)
    ab = ab_ref[0]
    a_bc = _split3_dot(ab, sel_a)
    b_bc = _split3_dot(ab, sel_b)
    gp = _split3_dot(gp_ref[...], sel_a)
    a_log = gp[0:1, :]
    dt_bias = gp[1:2, :]
    xs = a_bc + dt_bias
    softplus = jnp.maximum(xs, 0.0) + jnp.log1p(jnp.exp(-jnp.abs(xs)))
    g = -jnp.exp(a_log) * softplus
    beta = _sigmoid(b_bc)

    rin = jnp.bitwise_and(row, C - 1)
    gc = g
    sh = 1
    while sh < C:
        gc = gc + jnp.where(rin >= sh, pltpu.roll(gc, sh, 0), 0.0)
        sh *= 2

    gc3 = gc.reshape(NC, C, LANES)
    glast = gc3[:, C - 1:C, :]
    eg = jnp.exp(gc)
    kb = k * beta
    rhs_w = (kb * eg).reshape(NC, C, GDN_DK)
    rhs_u = (v * beta).reshape(NC, C, GDN_DV)
    q_dec = (q * eg).reshape(NC, C, GDN_DK)
    k3 = k.reshape(NC, C, GDN_DK)
    k_dec = k3 * jnp.exp(glast - gc3)

    k3b = k3.astype(BF16)
    kk = _bdot_nt(kb.reshape(NC, C, GDN_DK).astype(BF16), k3b)
    qk = _bdot_nt(q.reshape(NC, C, GDN_DK).astype(BF16), k3b)

    ii = lax.broadcasted_iota(jnp.int32, (C, C), 0)
    jj = lax.broadcasted_iota(jnp.int32, (C, C), 1)
    decay = jnp.exp(jnp.minimum(gc3 - jnp.swapaxes(gc3, 1, 2), 0.0))
    a_mat = jnp.where(ii > jj, kk * decay, 0.0)
    aqk = jnp.where(ii >= jj, qk * decay, 0.0)

    eye = (ii == jj).astype(F32)
    p = eye - jnp.where((ii >> 1) == (jj >> 1), a_mat, 0.0)
    lvl = 1
    while (1 << lvl) < C:
        in_parent = (ii >> (lvl + 1)) == (jj >> (lvl + 1))
        in_child = (ii >> lvl) == (jj >> lvl)
        e = jnp.where(jnp.logical_and(in_parent, jnp.logical_not(in_child)), a_mat, 0.0)
        pb = p.astype(BF16)
        pe = _bdot(pb, e.astype(BF16))
        p = p - _bdot(pe.astype(BF16), pb)
        lvl += 1

    rhs = jnp.concatenate([rhs_w, rhs_u], axis=-1).astype(BF16)
    wu = _bdot(p.astype(BF16), rhs)
    wq_s[:, 0:C, :] = wu[:, :, 0:GDN_DK].astype(BF16)
    wq_s[:, C:2 * C, :] = q_dec.astype(BF16)
    u_s[...] = wu[:, :, GDN_DK:GDN_DK + GDN_DV]
    aqk_s[...] = aqk.astype(BF16)
    kdt_s[...] = jnp.swapaxes(k_dec, 1, 2).astype(BF16)
    gl_s[...] = jnp.broadcast_to(jnp.exp(glast), (NC, 8, LANES))

    def chunk_step(c, state):
        sb = state.astype(BF16)
        ws = _dot(wq_s[c], sb)
        v_new = u_s[c] - ws[0:C, :]
        vb = v_new.astype(BF16)
        o_s[c] = ws[C:2 * C, :] + _dot(aqk_s[c], vb)
        return state * gl_s[c][0:1, :] + _dot(kdt_s[c], vb)

    lax.fori_loop(0, NC, chunk_step, jnp.zeros((GDN_DK, GDN_DV), F32))

    o = o_s[...].reshape(S, GDN_DV)
    ms = jnp.mean(o * o, axis=-1, keepdims=True)
    o = o * lax.rsqrt(ms + NORM_EPS) * onorm_ref[...]
    o_ref[0] = (o * _silu(z_ref[0].astype(F32))).astype(o_ref.dtype)


def _gdn(gdn3, ab3, conv_w, gparams, onorm):
    B, S, _ = gdn3.shape
    NC = S // GDN_CHUNK
    H = GDN_HEADS
    seq_spec = lambda off: pl.BlockSpec((1, S, LANES), lambda b, h, off=off: (b, 0, off + h))
    cw_spec = lambda off: pl.BlockSpec((GDN_CONV, LANES), lambda b, h, off=off: (0, off + h))
    return pl.pallas_call(
        _gdn_kernel,
        grid=(B, H),
        in_specs=[
            seq_spec(0), seq_spec(H), seq_spec(2 * H), seq_spec(3 * H),
            pl.BlockSpec((1, S, LANES), lambda b, h: (b, 0, 0)),
            cw_spec(0), cw_spec(H), cw_spec(2 * H),
            pl.BlockSpec((8, LANES), lambda b, h: (0, 0)),
            pl.BlockSpec((1, LANES), lambda b, h: (0, 0)),
        ],
        out_specs=pl.BlockSpec((1, S, LANES), lambda b, h: (b, 0, h)),
        out_shape=jax.ShapeDtypeStruct((B, S, GDN_V), BF16),
        scratch_shapes=[
            pltpu.VMEM((NC, 2 * GDN_CHUNK, GDN_DK), BF16),
            pltpu.VMEM((NC, GDN_CHUNK, GDN_DV), F32),
            pltpu.VMEM((NC, GDN_CHUNK, GDN_CHUNK), BF16),
            pltpu.VMEM((NC, GDN_DK, GDN_CHUNK), BF16),
            pltpu.VMEM((NC, 8, LANES), F32),
            pltpu.VMEM((NC, GDN_CHUNK, GDN_DV), F32),
        ],
        compiler_params=pltpu.CompilerParams(
            dimension_semantics=("arbitrary", "arbitrary"), vmem_limit_bytes=VMEM_LIMIT),
        name="gdn",
    )(gdn3, gdn3, gdn3, gdn3, ab3, conv_w, conv_w, conv_w, gparams, onorm)


_ATT_BLOCK = 256


def _diffattn_kernel(lam_ref, subln_ref, q_ref, k_ref, v_ref, o_ref, s_scr, p_scr, vext_scr,
                     *, lambda_init):
    S = q_ref.shape[1]
    tq = min(_ATT_BLOCK, S)
    dv = 2 * DIFF_DH
    neg = jnp.finfo(F32).min

    lp = lam_ref[...]
    lam = (jnp.exp(jnp.sum(lp[0:1, :] * lp[1:2, :], axis=-1, keepdims=True))
           - jnp.exp(jnp.sum(lp[2:3, :] * lp[3:4, :], axis=-1, keepdims=True))
           + lambda_init)

    vext_scr[:, 0:dv] = v_ref[0]
    vext_scr[:, dv:2 * dv] = jnp.ones((S, dv), BF16)

    lane = lax.broadcasted_iota(jnp.int32, (tq, dv), 1)
    causal = (lax.broadcasted_iota(jnp.int32, (2 * tq, tq), 1)
              <= jnp.bitwise_and(lax.broadcasted_iota(jnp.int32, (2 * tq, tq), 0), tq - 1))

    for qb in range(S // tq):
        r0 = qb * tq
        n = r0 + tq
        qs = q_ref[0, r0:n, :] * (DIFF_DH ** -0.5)
        zero = jnp.zeros_like(qs)
        qq = jnp.concatenate([jnp.where(lane < DIFF_DH, qs, zero),
                              jnp.where(lane >= DIFF_DH, qs, zero)], axis=0)

        m_run = None
        for c0 in range(0, n, tq):
            s = _dot_nt(qq, k_ref[0, c0:c0 + tq, :])
            if c0 == r0:
                s = jnp.where(causal, s, neg)
            s_scr[:, c0:c0 + tq] = s
            for t in range(0, tq, LANES):
                tile = s[:, t:t + LANES]
                m_run = tile if m_run is None else jnp.maximum(m_run, tile)
        m_b = jnp.broadcast_to(jnp.max(m_run, axis=-1, keepdims=True), (2 * tq, LANES))

        for c0 in range(0, n, tq):
            for t in range(c0, c0 + tq, LANES):
                p_scr[:, t:t + LANES] = jnp.exp(s_scr[:, t:t + LANES] - m_b).astype(BF16)

        oe = _dot(p_scr[:, 0:n], vext_scr[0:n, :])
        a = (oe[0:tq, 0:dv] / oe[0:tq, dv:2 * dv]
             - lam * (oe[tq:2 * tq, 0:dv] / oe[tq:2 * tq, dv:2 * dv]))
        ms = jnp.mean(a * a, axis=-1, keepdims=True)
        a = a * lax.rsqrt(ms + NORM_EPS) * subln_ref[...] * (1.0 - lambda_init)
        o_ref[0, r0:n, :] = a.astype(o_ref.dtype)


def _diffattn(dqkv3, lam_params, subln, lambda_init):
    B, S, _ = dqkv3.shape
    H = DIFF_HEADS
    tq = min(_ATT_BLOCK, S)
    dv = 2 * DIFF_DH
    head = lambda off: pl.BlockSpec((1, S, dv), lambda b, h, off=off: (b, 0, off + h))
    return pl.pallas_call(
        functools.partial(_diffattn_kernel, lambda_init=lambda_init),
        grid=(B, H),
        in_specs=[
            pl.BlockSpec((8, DIFF_DH), lambda b, h: (0, 0)),
            pl.BlockSpec((1, dv), lambda b, h: (0, 0)),
            head(0), head(H), head(2 * H),
        ],
        out_specs=pl.BlockSpec((1, S, dv), lambda b, h: (b, 0, h)),
        out_shape=jax.ShapeDtypeStruct((B, S, DIFF_V), BF16),
        scratch_shapes=[
            pltpu.VMEM((2 * tq, S), F32),
            pltpu.VMEM((2 * tq, S), BF16),
            pltpu.VMEM((S, 2 * dv), BF16),
        ],
        compiler_params=pltpu.CompilerParams(
            dimension_semantics=("arbitrary", "arbitrary"), vmem_limit_bytes=VMEM_LIMIT),
        name="diffattn",
    )(lam_params, subln, dqkv3, dqkv3, dqkv3)


def _merge_kernel(x_ref, og_ref, od_ref, gates_ref, wbg_ref, wbd_ref, wout_ref,
                  npost_ref, npre_ref, x1_ref, h2_ref):
    mg = _dot(og_ref[...], wbg_ref[...])
    md = _dot(od_ref[...], wbd_ref[...])
    gg = _sigmoid(gates_ref[:, 0:D_MODEL].astype(F32))
    gd = _sigmoid(gates_ref[:, D_MODEL:2 * D_MODEL].astype(F32))
    merged = (gg * mg + gd * md).astype(BF16)
    m = _dot(merged, wout_ref[...])
    ms = jnp.mean(m * m, axis=-1, keepdims=True)
    x1 = x_ref[...] + m * lax.rsqrt(ms + NORM_EPS) * npost_ref[...]
    x1_ref[...] = x1
    ms1 = jnp.mean(x1 * x1, axis=-1, keepdims=True)
    h2_ref[...] = (x1 * lax.rsqrt(ms1 + NORM_EPS) * npre_ref[...]).astype(BF16)


def _merge(x2, og, od, gates, wbg, wbd, wout, npost, npre, tm):
    T = x2.shape[0]
    row = lambda w: pl.BlockSpec((tm, w), lambda i: (i, 0))
    return pl.pallas_call(
        _merge_kernel,
        grid=(T // tm,),
        in_specs=[
            row(D_MODEL), row(GDN_V), row(DIFF_V), row(2 * D_MODEL),
            _const_spec((GDN_V, D_MODEL)), _const_spec((DIFF_V, D_MODEL)),
            _const_spec((D_MODEL, D_MODEL)),
            _const_spec((1, D_MODEL)), _const_spec((1, D_MODEL)),
        ],
        out_specs=[row(D_MODEL), row(D_MODEL)],
        out_shape=[jax.ShapeDtypeStruct((T, D_MODEL), F32),
                   jax.ShapeDtypeStruct((T, D_MODEL), BF16)],
        compiler_params=pltpu.CompilerParams(
            dimension_semantics=("arbitrary",), vmem_limit_bytes=VMEM_LIMIT),
        name="merge",
    )(x2, og, od, gates, wbg, wbd, wout, npost, npre)


_FFN_CHUNK = 256
_HALO = 8


def _ffn_kernel(x1_ref, h2_ref, wup_ref, cw_ref, cb_ref, wdown_ref, npost_ref, out_ref,
                ubuf, carry, facc):
    si = pl.program_id(1)
    tm = h2_ref.shape[1]
    n_chunks = D_FF // _FFN_CHUNK

    @pl.when(si == 0)
    def _():
        carry[...] = jnp.zeros_like(carry)

    h2 = h2_ref[0]

    def conv_half(col0):
        u = _dot(h2, wup_ref[:, col0:col0 + _FFN_CHUNK])
        ubuf[0:_HALO, :] = carry[:, col0:col0 + _FFN_CHUNK]
        ubuf[_HALO:_HALO + tm, :] = u
        carry[:, col0:col0 + _FFN_CHUNK] = ubuf[tm:tm + _HALO, :]
        w = cw_ref[:, col0:col0 + _FFN_CHUNK]
        y = u * w[FFN_CONV - 1:FFN_CONV, :]
        for sh in range(1, FFN_CONV):
            y = y + ubuf[_HALO - sh:_HALO - sh + tm, :] * w[FFN_CONV - 1 - sh:FFN_CONV - sh, :]
        return y + cb_ref[:, col0:col0 + _FFN_CHUNK]

    for c in range(n_chunks):
        gate = conv_half(c * _FFN_CHUNK)
        up = conv_half(D_FF + c * _FFN_CHUNK)
        act = (_silu(gate) * up).astype(BF16)
        part = _dot(act, wdown_ref[c * _FFN_CHUNK:(c + 1) * _FFN_CHUNK, :])
        if c == 0:
            facc[...] = part
        else:
            facc[...] += part

    f = facc[...]
    ms = jnp.mean(f * f, axis=-1, keepdims=True)
    out_ref[0] = x1_ref[0] + f * lax.rsqrt(ms + NORM_EPS) * npost_ref[...]


def _ffn(x1_3, h2_3, wup, cw, cb, wdown, npost, tm):
    B, S, _ = x1_3.shape
    seq = lambda: pl.BlockSpec((1, tm, D_MODEL), lambda b, s: (b, s, 0))
    return pl.pallas_call(
        _ffn_kernel,
        grid=(B, S // tm),
        in_specs=[
            seq(), seq(),
            _const_spec((D_MODEL, 2 * D_FF)),
            _const_spec((FFN_CONV, 2 * D_FF)),
            _const_spec((1, 2 * D_FF)),
            _const_spec((D_FF, D_MODEL)),
            _const_spec((1, D_MODEL)),
        ],
        out_specs=seq(),
        out_shape=jax.ShapeDtypeStruct((B, S, D_MODEL), F32),
        scratch_shapes=[
            pltpu.VMEM((tm + _HALO, _FFN_CHUNK), F32),
            pltpu.VMEM((_HALO, 2 * D_FF), F32),
            pltpu.VMEM((tm, D_MODEL), F32),
        ],
        compiler_params=pltpu.CompilerParams(
            dimension_semantics=("arbitrary", "arbitrary"), vmem_limit_bytes=VMEM_LIMIT),
        name="convffn",
    )(x1_3, h2_3, wup, cw, cb, wdown, npost)


def _row_tile(n, want):
    t = min(want, n)
    while n % t:
        t //= 2
    return t


def _layer(x, lambda_init, norm_mix_pre, w_in, conv_qkv_w, gdn_A_log, gdn_dt_bias, gdn_out_norm,
           lambda_q1, lambda_k1, lambda_q2, lambda_k2, diff_subln, w_branch_gdn, w_branch_diff,
           w_out, norm_mix_post, norm_ffn_pre, w_up, ffn_conv_w, ffn_conv_b, w_down, norm_ffn_post):
    B, S, D = x.shape
    T = B * S
    x2 = x.reshape(T, D)
    row2 = lambda v: v.reshape(1, -1).astype(F32)

    n_gdn = 4 * GDN_QK
    w_ab = jnp.pad(w_in[:, n_gdn:n_gdn + 2 * GDN_HEADS], ((0, 0), (0, LANES - 2 * GDN_HEADS)))
    w_all = jnp.concatenate([w_in[:, :n_gdn], w_ab, w_in[:, n_gdn + 2 * GDN_HEADS:]],
                            axis=1).astype(BF16)

    gdn, ab, dqkv, gates = _inproj(x2, row2(norm_mix_pre), w_all, _row_tile(T, 512))

    gparams = jnp.zeros((8, LANES), F32)
    gparams = gparams.at[0, :GDN_HEADS].set(gdn_A_log.astype(F32))
    gparams = gparams.at[1, :GDN_HEADS].set(gdn_dt_bias.astype(F32))
    o_gdn = _gdn(gdn.reshape(B, S, -1), ab.reshape(B, S, LANES), conv_qkv_w.astype(F32),
                 gparams, row2(gdn_out_norm))

    lam_params = jnp.zeros((8, DIFF_DH), F32)
    lam_params = lam_params.at[0].set(lambda_q1.astype(F32)).at[1].set(lambda_k1.astype(F32))
    lam_params = lam_params.at[2].set(lambda_q2.astype(F32)).at[3].set(lambda_k2.astype(F32))
    o_diff = _diffattn(dqkv.reshape(B, S, -1), lam_params, row2(diff_subln), lambda_init)

    x1, h2 = _merge(x2, o_gdn.reshape(T, -1), o_diff.reshape(T, -1), gates,
                    w_branch_gdn.astype(BF16), w_branch_diff.astype(BF16), w_out.astype(BF16),
                    row2(norm_mix_post), row2(norm_ffn_pre), _row_tile(T, 512))

    out = _ffn(x1.reshape(B, S, D), h2.reshape(B, S, D), w_up.astype(BF16),
               ffn_conv_w.astype(F32), row2(ffn_conv_b), w_down.astype(BF16),
               row2(norm_ffn_post), _row_tile(S, 512))
    return out


def kernel(x, norm_mix_pre, w_in, conv_qkv_w, gdn_A_log, gdn_dt_bias, gdn_out_norm, lambda_q1, lambda_k1, lambda_q2, lambda_k2, diff_subln, w_branch_gdn, w_branch_diff, w_out, norm_mix_post, norm_ffn_pre, w_up, ffn_conv_w, ffn_conv_b, w_down, norm_ffn_post):
    depth = w_in.shape[0]
    for l in range(depth):
        lambda_init = 0.8 - 0.6 * math.exp(-0.3 * l)
        x = _layer(x, lambda_init, norm_mix_pre[l], w_in[l], conv_qkv_w[l], gdn_A_log[l],
                   gdn_dt_bias[l], gdn_out_norm[l], lambda_q1[l], lambda_k1[l], lambda_q2[l],
                   lambda_k2[l], diff_subln[l], w_branch_gdn[l], w_branch_diff[l], w_out[l],
                   norm_mix_post[l], norm_ffn_pre[l], w_up[l], ffn_conv_w[l], ffn_conv_b[l],
                   w_down[l], norm_ffn_post[l])
    return x
```

```python
import functools
import math

import jax
import jax.numpy as jnp
from jax import lax
from jax.experimental import pallas as pl
from jax.experimental.pallas import tpu as pltpu

F32 = jnp.float32
BF16 = jnp.bfloat16

D_MODEL = 1024
GDN_HEADS = 8
GDN_DK = 128
GDN_DV = 128
GDN_CONV = 4
DIFF_HEADS = 8
DIFF_DH = 64
D_FF = 2816
FFN_CONV = 3
NORM_EPS = 1e-6

GDN_QK = GDN_HEADS * GDN_DK
GDN_V = GDN_HEADS * GDN_DV
DIFF_QK = DIFF_HEADS * 2 * DIFF_DH
DIFF_V = DIFF_HEADS * 2 * DIFF_DH

LANES = 128
GDN_CHUNK = 128
VMEM_LIMIT = 56 * 1024 * 1024


def _dot(a, b):
    return jnp.dot(a, b, preferred_element_type=F32)


def _dot_nt(a, b):
    return lax.dot_general(a, b, (((1,), (1,)), ((), ())), preferred_element_type=F32)


def _bdot(a, b):
    return lax.dot_general(a, b, (((2,), (1,)), ((0,), (0,))), preferred_element_type=F32)


def _bdot_nt(a, b):
    return lax.dot_general(a, b, (((2,), (2,)), ((0,), (0,))), preferred_element_type=F32)


def _sigmoid(x):
    return 1.0 / (1.0 + jnp.exp(-x))


def _silu(x):
    return x * _sigmoid(x)


def _const_spec(shape):
    nd = len(shape)
    return pl.BlockSpec(shape, lambda *_: (0,) * nd, pipeline_mode=pl.Buffered(1))


_IN_CHUNK = 512


def _inproj_kernel(x_ref, g_ref, w_ref, gdn_ref, ab_ref, dqkv_ref, gates_ref, h_scr):
    x = x_ref[...]
    ms = jnp.mean(x * x, axis=-1, keepdims=True)
    h_scr[...] = (x * lax.rsqrt(ms + NORM_EPS) * g_ref[...]).astype(BF16)

    def emit(out_ref, w_off, width):
        for c0 in range(0, width, _IN_CHUNK):
            cw = min(_IN_CHUNK, width - c0)
            acc = _dot(h_scr[...], w_ref[:, w_off + c0:w_off + c0 + cw])
            out_ref[:, c0:c0 + cw] = acc.astype(out_ref.dtype)

    emit(gdn_ref, 0, 4 * GDN_QK)
    emit(ab_ref, 4 * GDN_QK, LANES)
    emit(dqkv_ref, 4 * GDN_QK + LANES, 3 * DIFF_QK)
    emit(gates_ref, 4 * GDN_QK + LANES + 3 * DIFF_QK, 2 * D_MODEL)


def _inproj(x2, g, w_all, tm):
    T = x2.shape[0]
    n_all = w_all.shape[1]
    return pl.pallas_call(
        _inproj_kernel,
        grid=(T // tm,),
        in_specs=[
            pl.BlockSpec((tm, D_MODEL), lambda i: (i, 0)),
            _const_spec((1, D_MODEL)),
            _const_spec((D_MODEL, n_all)),
        ],
        out_specs=[
            pl.BlockSpec((tm, 4 * GDN_QK), lambda i: (i, 0)),
            pl.BlockSpec((tm, LANES), lambda i: (i, 0)),
            pl.BlockSpec((tm, 3 * DIFF_QK), lambda i: (i, 0)),
            pl.BlockSpec((tm, 2 * D_MODEL), lambda i: (i, 0)),
        ],
        out_shape=[
            jax.ShapeDtypeStruct((T, 4 * GDN_QK), BF16),
            jax.ShapeDtypeStruct((T, LANES), F32),
            jax.ShapeDtypeStruct((T, 3 * DIFF_QK), BF16),
            jax.ShapeDtypeStruct((T, 2 * D_MODEL), BF16),
        ],
        scratch_shapes=[pltpu.VMEM((tm, D_MODEL), BF16)],
        compiler_params=pltpu.CompilerParams(
            dimension_semantics=("arbitrary",), vmem_limit_bytes=VMEM_LIMIT),
        name="inproj",
    )(x2, g, w_all)


def _split3_dot(x, sel):
    hi = x.astype(BF16)
    r1 = x - hi.astype(F32)
    mid = r1.astype(BF16)
    lo = (r1 - mid.astype(F32)).astype(BF16)
    return _dot(hi, sel) + _dot(mid, sel) + _dot(lo, sel)


def _gdn_kernel(q_ref, k_ref, v_ref, z_ref, ab_ref, cwq_ref, cwk_ref, cwv_ref, gp_ref, onorm_ref,
                o_ref, wq_s, u_s, aqk_s, kdt_s, gl_s, o_s):
    head = pl.program_id(1)
    S = q_ref.shape[1]
    C = GDN_CHUNK
    NC = S // C
    row = lax.broadcasted_iota(jnp.int32, (S, LANES), 0)

    def conv_silu(x_ref, w_ref):
        x = x_ref[0].astype(F32)
        w = w_ref[...]
        y = x * w[GDN_CONV - 1:GDN_CONV, :]
        for sh in range(1, GDN_CONV):
            xs = jnp.where(row >= sh, pltpu.roll(x, sh, 0), 0.0)
            y = y + xs * w[GDN_CONV - 1 - sh:GDN_CONV - sh, :]
        return _silu(y)

    def l2norm(x):
        return x * lax.rsqrt(jnp.sum(x * x, axis=-1, keepdims=True) + NORM_EPS)

    q = l2norm(conv_silu(q_ref, cwq_ref)) * (GDN_DK ** -0.5)
    k = l2norm(conv_silu(k_ref, cwk_ref))
    v = conv_silu(v_ref, cwv_ref)

    ri = lax.broadcasted_iota(jnp.int32, (LANES, LANES), 0)
    sel_a = (ri == head).astype(BF16)
    sel_b = (ri == head + GDN_HEADS).astype(BF16)
    ab = ab_ref[0]
    a_bc = _split3_dot(ab, sel_a)
    b_bc = _split3_dot(ab, sel_b)
    gp = _split3_dot(gp_ref[...], sel_a)
    a_log = gp[0:1, :]
    dt_bias = gp[1:2, :]
    xs = a_bc + dt_bias
    softplus = jnp.maximum(xs, 0.0) + jnp.log1p(jnp.exp(-jnp.abs(xs)))
    g = -jnp.exp(a_log) * softplus
    beta = _sigmoid(b_bc)

    rin = jnp.bitwise_and(row, C - 1)
    gc = g
    sh = 1
    while sh < C:
        gc = gc + jnp.where(rin >= sh, pltpu.roll(gc, sh, 0), 0.0)
        sh *= 2

    gc3 = gc.reshape(NC, C, LANES)
    glast = gc3[:, C - 1:C, :]
    eg = jnp.exp(gc)
    kb = k * beta
    rhs_w = (kb * eg).reshape(NC, C, GDN_DK)
    rhs_u = (v * beta).reshape(NC, C, GDN_DV)
    q_dec = (q * eg).reshape(NC, C, GDN_DK)
    k3 = k.reshape(NC, C, GDN_DK)
    k_dec = k3 * jnp.exp(glast - gc3)

    k3b = k3.astype(BF16)
    kk = _bdot_nt(kb.reshape(NC, C, GDN_DK).astype(BF16), k3b)
    qk = _bdot_nt(q.reshape(NC, C, GDN_DK).astype(BF16), k3b)

    ii = lax.broadcasted_iota(jnp.int32, (C, C), 0)
    jj = lax.broadcasted_iota(jnp.int32, (C, C), 1)
    decay = jnp.exp(jnp.minimum(gc3 - jnp.swapaxes(gc3, 1, 2), 0.0))
    a_mat = jnp.where(ii > jj, kk * decay, 0.0)
    aqk = jnp.where(ii >= jj, qk * decay, 0.0)

    eye = (ii == jj).astype(F32)
    p = eye - jnp.where((ii >> 1) == (jj >> 1), a_mat, 0.0)
    lvl = 1
    while (1 << lvl) < C:
        in_parent = (ii >> (lvl + 1)) == (jj >> (lvl + 1))
        in_child = (ii >> lvl) == (jj >> lvl)
        e = jnp.where(jnp.logical_and(in_parent, jnp.logical_not(in_child)), a_mat, 0.0)
        pb = p.astype(BF16)
        pe = _bdot(pb, e.astype(BF16))
        p = p - _bdot(pe.astype(BF16), pb)
        lvl += 1

    rhs = jnp.concatenate([rhs_w, rhs_u], axis=-1).astype(BF16)
    wu = _bdot(p.astype(BF16), rhs)
    wq_s[:, 0:C, :] = wu[:, :, 0:GDN_DK].astype(BF16)
    wq_s[:, C:2 * C, :] = q_dec.astype(BF16)
    u_s[...] = wu[:, :, GDN_DK:GDN_DK + GDN_DV]
    aqk_s[...] = aqk.astype(BF16)
    kdt_s[...] = jnp.swapaxes(k_dec, 1, 2).astype(BF16)
    gl_s[...] = jnp.broadcast_to(jnp.exp(glast), (NC, 8, LANES))

    def chunk_step(c, state):
        sb = state.astype(BF16)
        ws = _dot(wq_s[c], sb)
        v_new = u_s[c] - ws[0:C, :]
        vb = v_new.astype(BF16)
        o_s[c] = ws[C:2 * C, :] + _dot(aqk_s[c], vb)
        return state * gl_s[c][0:1, :] + _dot(kdt_s[c], vb)

    lax.fori_loop(0, NC, chunk_step, jnp.zeros((GDN_DK, GDN_DV), F32))

    o = o_s[...].reshape(S, GDN_DV)
    ms = jnp.mean(o * o, axis=-1, keepdims=True)
    o = o * lax.rsqrt(ms + NORM_EPS) * onorm_ref[...]
    o_ref[0] = (o * _silu(z_ref[0].astype(F32))).astype(o_ref.dtype)


def _gdn(gdn3, ab3, conv_w, gparams, onorm):
    B, S, _ = gdn3.shape
    NC = S // GDN_CHUNK
    H = GDN_HEADS
    seq_spec = lambda off: pl.BlockSpec((1, S, LANES), lambda b, h, off=off: (b, 0, off + h))
    cw_spec = lambda off: pl.BlockSpec((GDN_CONV, LANES), lambda b, h, off=off: (0, off + h))
    return pl.pallas_call(
        _gdn_kernel,
        grid=(B, H),
        in_specs=[
            seq_spec(0), seq_spec(H), seq_spec(2 * H), seq_spec(3 * H),
            pl.BlockSpec((1, S, LANES), lambda b, h: (b, 0, 0)),
            cw_spec(0), cw_spec(H), cw_spec(2 * H),
            pl.BlockSpec((8, LANES), lambda b, h: (0, 0)),
            pl.BlockSpec((1, LANES), lambda b, h: (0, 0)),
        ],
        out_specs=pl.BlockSpec((1, S, LANES), lambda b, h: (b, 0, h)),
        out_shape=jax.ShapeDtypeStruct((B, S, GDN_V), BF16),
        scratch_shapes=[
            pltpu.VMEM((NC, 2 * GDN_CHUNK, GDN_DK), BF16),
            pltpu.VMEM((NC, GDN_CHUNK, GDN_DV), F32),
            pltpu.VMEM((NC, GDN_CHUNK, GDN_CHUNK), BF16),
            pltpu.VMEM((NC, GDN_DK, GDN_CHUNK), BF16),
            pltpu.VMEM((NC, 8, LANES), F32),
            pltpu.VMEM((NC, GDN_CHUNK, GDN_DV), F32),
        ],
        compiler_params=pltpu.CompilerParams(
            dimension_semantics=("arbitrary", "arbitrary"), vmem_limit_bytes=VMEM_LIMIT),
        name="gdn",
    )(gdn3, gdn3, gdn3, gdn3, ab3, conv_w, conv_w, conv_w, gparams, onorm)


_ATT_BLOCK = 256


def _diffattn_kernel(lam_ref, subln_ref, q_ref, k_ref, v_ref, o_ref, s_scr, p_scr, vext_scr,
                     *, lambda_init):
    S = q_ref.shape[1]
    tq = min(_ATT_BLOCK, S)
    dv = 2 * DIFF_DH
    neg = jnp.finfo(F32).min

    lp = lam_ref[...]
    lam = (jnp.exp(jnp.sum(lp[0:1, :] * lp[1:2, :], axis=-1, keepdims=True))
           - jnp.exp(jnp.sum(lp[2:3, :] * lp[3:4, :], axis=-1, keepdims=True))
           + lambda_init)

    vext_scr[:, 0:dv] = v_ref[0]
    vext_scr[:, dv:2 * dv] = jnp.ones((S, dv), BF16)

    lane = lax.broadcasted_iota(jnp.int32, (tq, dv), 1)
    causal = (lax.broadcasted_iota(jnp.int32, (2 * tq, tq), 1)
              <= jnp.bitwise_and(lax.broadcasted_iota(jnp.int32, (2 * tq, tq), 0), tq - 1))

    for qb in range(S // tq):
        r0 = qb * tq
        n = r0 + tq
        qs = q_ref[0, r0:n, :] * (DIFF_DH ** -0.5)
        zero = jnp.zeros_like(qs)
        qq = jnp.concatenate([jnp.where(lane < DIFF_DH, qs, zero),
                              jnp.where(lane >= DIFF_DH, qs, zero)], axis=0)

        m_run = None
        for c0 in range(0, n, tq):
            s = _dot_nt(qq, k_ref[0, c0:c0 + tq, :])
            if c0 == r0:
                s = jnp.where(causal, s, neg)
            s_scr[:, c0:c0 + tq] = s
            for t in range(0, tq, LANES):
                tile = s[:, t:t + LANES]
                m_run = tile if m_run is None else jnp.maximum(m_run, tile)
        m_b = jnp.broadcast_to(jnp.max(m_run, axis=-1, keepdims=True), (2 * tq, LANES))

        for c0 in range(0, n, tq):
            for t in range(c0, c0 + tq, LANES):
                p_scr[:, t:t + LANES] = jnp.exp(s_scr[:, t:t + LANES] - m_b).astype(BF16)

        oe = _dot(p_scr[:, 0:n], vext_scr[0:n, :])
        a = (oe[0:tq, 0:dv] / oe[0:tq, dv:2 * dv]
             - lam * (oe[tq:2 * tq, 0:dv] / oe[tq:2 * tq, dv:2 * dv]))
        ms = jnp.mean(a * a, axis=-1, keepdims=True)
        a = a * lax.rsqrt(ms + NORM_EPS) * subln_ref[...] * (1.0 - lambda_init)
        o_ref[0, r0:n, :] = a.astype(o_ref.dtype)


def _diffattn(dqkv3, lam_params, subln, lambda_init):
    B, S, _ = dqkv3.shape
    H = DIFF_HEADS
    tq = min(_ATT_BLOCK, S)
    dv = 2 * DIFF_DH
    head = lambda off: pl.BlockSpec((1, S, dv), lambda b, h, off=off: (b, 0, off + h))
    return pl.pallas_call(
        functools.partial(_diffattn_kernel, lambda_init=lambda_init),
        grid=(B, H),
        in_specs=[
            pl.BlockSpec((8, DIFF_DH), lambda b, h: (0, 0)),
            pl.BlockSpec((1, dv), lambda b, h: (0, 0)),
            head(0), head(H), head(2 * H),
        ],
        out_specs=pl.BlockSpec((1, S, dv), lambda b, h: (b, 0, h)),
        out_shape=jax.ShapeDtypeStruct((B, S, DIFF_V), BF16),
        scratch_shapes=[
            pltpu.VMEM((2 * tq, S), F32),
            pltpu.VMEM((2 * tq, S), BF16),
            pltpu.VMEM((S, 2 * dv), BF16),
        ],
        compiler_params=pltpu.CompilerParams(
            dimension_semantics=("arbitrary", "arbitrary"), vmem_limit_bytes=VMEM_LIMIT),
        name="diffattn",
    )(lam_params, subln, dqkv3, dqkv3, dqkv3)


def _merge_kernel(x_ref, og_ref, od_ref, gates_ref, wbg_ref, wbd_ref, wout_ref,
                  npost_ref, npre_ref, x1_ref, h2_ref):
    mg = _dot(og_ref[...], wbg_ref[...])
    md = _dot(od_ref[...], wbd_ref[...])
    gg = _sigmoid(gates_ref[:, 0:D_MODEL].astype(F32))
    gd = _sigmoid(gates_ref[:, D_MODEL:2 * D_MODEL].astype(F32))
    merged = (gg * mg + gd * md).astype(BF16)
    m = _dot(merged, wout_ref[...])
    ms = jnp.mean(m * m, axis=-1, keepdims=True)
    x1 = x_ref[...] + m * lax.rsqrt(ms + NORM_EPS) * npost_ref[...]
    x1_ref[...] = x1
    ms1 = jnp.mean(x1 * x1, axis=-1, keepdims=True)
    h2_ref[...] = (x1 * lax.rsqrt(ms1 + NORM_EPS) * npre_ref[...]).astype(BF16)


def _merge(x2, og, od, gates, wbg, wbd, wout, npost, npre, tm):
    T = x2.shape[0]
    row = lambda w: pl.BlockSpec((tm, w), lambda i: (i, 0))
    return pl.pallas_call(
        _merge_kernel,
        grid=(T // tm,),
        in_specs=[
            row(D_MODEL), row(GDN_V), row(DIFF_V), row(2 * D_MODEL),
            _const_spec((GDN_V, D_MODEL)), _const_spec((DIFF_V, D_MODEL)),
            _const_spec((D_MODEL, D_MODEL)),
            _const_spec((1, D_MODEL)), _const_spec((1, D_MODEL)),
        ],
        out_specs=[row(D_MODEL), row(D_MODEL)],
        out_shape=[jax.ShapeDtypeStruct((T, D_MODEL), F32),
                   jax.ShapeDtypeStruct((T, D_MODEL), BF16)],
        compiler_params=pltpu.CompilerParams(
            dimension_semantics=("arbitrary",), vmem_limit_bytes=VMEM_LIMIT),
        name="merge",
    )(x2, og, od, gates, wbg, wbd, wout, npost, npre)


_FFN_CHUNK = 512
_HALO = 8


def _ffn_kernel(x1_ref, h2_ref, wup_ref, cw_ref, cb_ref, wdown_ref, npost_ref, out_ref,
                carry, act_scr):
    si = pl.program_id(1)
    tm = h2_ref.shape[1]

    @pl.when(si == 0)
    def _():
        carry[...] = jnp.zeros_like(carry)

    h2 = h2_ref[0]

    def conv_half(col0, cw):
        u = _dot(h2, wup_ref[:, col0:col0 + cw])
        prev = carry[:, col0:col0 + cw]
        carry[:, col0:col0 + cw] = u[tm - _HALO:tm, :]
        w = cw_ref[:, col0:col0 + cw]
        row = lax.broadcasted_iota(jnp.int32, (_HALO, cw), 0)
        y = u * w[FFN_CONV - 1:FFN_CONV, :]
        for sh in range(1, FFN_CONV):
            r = pltpu.roll(u, sh, 0)
            head = jnp.where(row < sh, pltpu.roll(prev, sh, 0), r[0:_HALO, :])
            r = jnp.concatenate([head, r[_HALO:, :]], axis=0)
            y = y + r * w[FFN_CONV - 1 - sh:FFN_CONV - sh, :]
        return y + cb_ref[:, col0:col0 + cw]

    for c0 in range(0, D_FF, _FFN_CHUNK):
        cw = min(_FFN_CHUNK, D_FF - c0)
        gate = conv_half(c0, cw)
        up = conv_half(D_FF + c0, cw)
        act_scr[:, c0:c0 + cw] = (_silu(gate) * up).astype(BF16)

    f = _dot(act_scr[...], wdown_ref[...])
    ms = jnp.mean(f * f, axis=-1, keepdims=True)
    out_ref[0] = x1_ref[0] + f * lax.rsqrt(ms + NORM_EPS) * npost_ref[...]


def _ffn(x1_3, h2_3, wup, cw, cb, wdown, npost, tm):
    B, S, _ = x1_3.shape
    seq = lambda: pl.BlockSpec((1, tm, D_MODEL), lambda b, s: (b, s, 0))
    return pl.pallas_call(
        _ffn_kernel,
        grid=(B, S // tm),
        in_specs=[
            seq(), seq(),
            _const_spec((D_MODEL, 2 * D_FF)),
            _const_spec((FFN_CONV, 2 * D_FF)),
            _const_spec((1, 2 * D_FF)),
            _const_spec((D_FF, D_MODEL)),
            _const_spec((1, D_MODEL)),
        ],
        out_specs=seq(),
        out_shape=jax.ShapeDtypeStruct((B, S, D_MODEL), F32),
        scratch_shapes=[
            pltpu.VMEM((_HALO, 2 * D_FF), F32),
            pltpu.VMEM((tm, D_FF), BF16),
        ],
        compiler_params=pltpu.CompilerParams(
            dimension_semantics=("arbitrary", "arbitrary"), vmem_limit_bytes=VMEM_LIMIT),
        name="convffn",
    )(x1_3, h2_3, wup, cw, cb, wdown, npost)


def _row_tile(n, want):
    t = min(want, n)
    while n % t:
        t //= 2
    return t


def _layer(x, lambda_init, norm_mix_pre, w_in, conv_qkv_w, gdn_A_log, gdn_dt_bias, gdn_out_norm,
           lambda_q1, lambda_k1, lambda_q2, lambda_k2, diff_subln, w_branch_gdn, w_branch_diff,
           w_out, norm_mix_post, norm_ffn_pre, w_up, ffn_conv_w, ffn_conv_b, w_down, norm_ffn_post):
    B, S, D = x.shape
    T = B * S
    x2 = x.reshape(T, D)
    row2 = lambda v: v.reshape(1, -1).astype(F32)

    n_gdn = 4 * GDN_QK
    w_ab = jnp.pad(w_in[:, n_gdn:n_gdn + 2 * GDN_HEADS], ((0, 0), (0, LANES - 2 * GDN_HEADS)))
    w_all = jnp.concatenate([w_in[:, :n_gdn], w_ab, w_in[:, n_gdn + 2 * GDN_HEADS:]],
                            axis=1).astype(BF16)

    gdn, ab, dqkv, gates = _inproj(x2, row2(norm_mix_pre), w_all, _row_tile(T, 512))

    gparams = jnp.zeros((8, LANES), F32)
    gparams = gparams.at[0, :GDN_HEADS].set(gdn_A_log.astype(F32))
    gparams = gparams.at[1, :GDN_HEADS].set(gdn_dt_bias.astype(F32))
    o_gdn = _gdn(gdn.reshape(B, S, -1), ab.reshape(B, S, LANES), conv_qkv_w.astype(F32),
                 gparams, row2(gdn_out_norm))

    lam_params = jnp.zeros((8, DIFF_DH), F32)
    lam_params = lam_params.at[0].set(lambda_q1.astype(F32)).at[1].set(lambda_k1.astype(F32))
    lam_params = lam_params.at[2].set(lambda_q2.astype(F32)).at[3].set(lambda_k2.astype(F32))
    o_diff = _diffattn(dqkv.reshape(B, S, -1), lam_params, row2(diff_subln), lambda_init)

    x1, h2 = _merge(x2, o_gdn.reshape(T, -1), o_diff.reshape(T, -1), gates,
                    w_branch_gdn.astype(BF16), w_branch_diff.astype(BF16), w_out.astype(BF16),
                    row2(norm_mix_post), row2(norm_ffn_pre), _row_tile(T, 512))

    out = _ffn(x1.reshape(B, S, D), h2.reshape(B, S, D), w_up.astype(BF16),
               ffn_conv_w.astype(F32), row2(ffn_conv_b), w_down.astype(BF16),
               row2(norm_ffn_post), _row_tile(S, 512))
    return out


def kernel(x, norm_mix_pre, w_in, conv_qkv_w, gdn_A_log, gdn_dt_bias, gdn_out_norm, lambda_q1, lambda_k1, lambda_q2, lambda_k2, diff_subln, w_branch_gdn, w_branch_diff, w_out, norm_mix_post, norm_ffn_pre, w_up, ffn_conv_w, ffn_conv_b, w_down, norm_ffn_post):
    depth = w_in.shape[0]
    for l in range(depth):
        lambda_init = 0.8 - 0.6 * math.exp(-0.3 * l)
        x = _layer(x, lambda_init, norm_mix_pre[l], w_in[l], conv_qkv_w[l], gdn_A_log[l],
                   gdn_dt_bias[l], gdn_out_norm[l], lambda_q1[l], lambda_k1[l], lambda_q2[l],
                   lambda_k2[l], diff_subln[l], w_branch_gdn[l], w_branch_diff[l], w_out[l],
                   norm_mix_post[l], norm_ffn_pre[l], w_up[l], ffn_conv_w[l], ffn_conv_b[l],
                   w_down[l], norm_ffn_post[l])
    return x
```

```python
import functools
import math

import jax
import jax.numpy as jnp
from jax import lax
from jax.experimental import pallas as pl
from jax.experimental.pallas import tpu as pltpu

F32 = jnp.float32
BF16 = jnp.bfloat16

D_MODEL = 1024
GDN_HEADS = 8
GDN_DK = 128
GDN_DV = 128
GDN_CONV = 4
DIFF_HEADS = 8
DIFF_DH = 64
D_FF = 2816
FFN_CONV = 3
NORM_EPS = 1e-6

GDN_QK = GDN_HEADS * GDN_DK
GDN_V = GDN_HEADS * GDN_DV
DIFF_QK = DIFF_HEADS * 2 * DIFF_DH
DIFF_V = DIFF_HEADS * 2 * DIFF_DH

LANES = 128
GDN_CHUNK = 128
VMEM_LIMIT = 56 * 1024 * 1024


def _dot(a, b):
    return jnp.dot(a, b, preferred_element_type=F32)


def _dot_nt(a, b):
    return lax.dot_general(a, b, (((1,), (1,)), ((), ())), preferred_element_type=F32)


def _bdot(a, b):
    return lax.dot_general(a, b, (((2,), (1,)), ((0,), (0,))), preferred_element_type=F32)


def _sigmoid(x):
    return 1.0 / (1.0 + jnp.exp(-x))


def _silu(x):
    return x * _sigmoid(x)


def _softplus(x):
    return jnp.maximum(x, 0.0) + jnp.log1p(jnp.exp(-jnp.abs(x)))


def _const_spec(shape):
    nd = len(shape)
    return pl.BlockSpec(shape, lambda *_: (0,) * nd, pipeline_mode=pl.Buffered(1))


_HALO = 8


def _causal_conv(u, prev, w, taps):
    row = lax.broadcasted_iota(jnp.int32, prev.shape, 0)
    y = u * w[taps - 1:taps, :]
    for sh in range(1, taps):
        r = pltpu.roll(u, sh, 0)
        head = jnp.where(row < sh, pltpu.roll(prev, sh, 0), r[0:_HALO, :])
        r = jnp.concatenate([head, r[_HALO:, :]], axis=0)
        y = y + r * w[taps - 1 - sh:taps - sh, :]
    return y


_IN_CHUNK = 512


def _inproj_kernel(x_ref, g_ref, w_ref, cw_ref, gp_ref, qkv_ref, z_ref, gb_ref, dqkv_ref, gates_ref,
                   h_scr, carry, *, tiles_per_seq):
    tm = x_ref.shape[0]

    @pl.when(pl.program_id(0) % tiles_per_seq == 0)
    def _():
        carry[...] = jnp.zeros_like(carry)

    x = x_ref[...]
    ms = jnp.mean(x * x, axis=-1, keepdims=True)
    h_scr[...] = (x * lax.rsqrt(ms + NORM_EPS) * g_ref[...]).astype(BF16)

    w_plain = 4 * GDN_QK + LANES
    plain = ([(dqkv_ref, c0, w_plain + c0) for c0 in range(0, 3 * DIFF_QK, _IN_CHUNK)]
             + [(gates_ref, c0, w_plain + 3 * DIFF_QK + c0) for c0 in range(0, 2 * D_MODEL, _IN_CHUNK)])

    def emit_plain(count):
        for _ in range(min(count, len(plain))):
            out_ref, c0, w_off = plain.pop(0)
            acc = _dot(h_scr[...], w_ref[:, w_off:w_off + _IN_CHUNK])
            out_ref[:, c0:c0 + _IN_CHUNK] = acc.astype(out_ref.dtype)

    for c0 in range(0, 3 * GDN_QK, _IN_CHUNK):
        emit_plain(1)
        u = _dot(h_scr[...], w_ref[:, c0:c0 + _IN_CHUNK])
        prev = carry[:, c0:c0 + _IN_CHUNK]
        carry[:, c0:c0 + _IN_CHUNK] = u[tm - _HALO:tm, :]
        y = _silu(_causal_conv(u, prev, cw_ref[:, c0:c0 + _IN_CHUNK], GDN_CONV))
        if c0 < 2 * GDN_QK:
            scale = GDN_DK ** -0.5 if c0 < GDN_QK else 1.0
            heads = []
            for h0 in range(0, _IN_CHUNK, GDN_DK):
                yh = y[:, h0:h0 + GDN_DK]
                inv = lax.rsqrt(jnp.sum(yh * yh, axis=-1, keepdims=True) + NORM_EPS)
                heads.append(yh * (inv * scale))
            y = jnp.concatenate(heads, axis=1)
        qkv_ref[:, c0:c0 + _IN_CHUNK] = y.astype(BF16)

    w_off = 3 * GDN_QK
    for c0 in range(0, GDN_V, _IN_CHUNK):
        emit_plain(1)
        zc = _dot(h_scr[...], w_ref[:, w_off + c0:w_off + c0 + _IN_CHUNK])
        z_ref[:, c0:c0 + _IN_CHUNK] = _silu(zc).astype(BF16)

    w_off += GDN_V
    ab = _dot(h_scr[...], w_ref[:, w_off:w_off + LANES])
    gp = gp_ref[...]
    g = -jnp.exp(gp[0:1, :]) * _softplus(ab + gp[1:2, :])
    lane = lax.broadcasted_iota(jnp.int32, ab.shape, 1)
    gb_ref[...] = jnp.where(lane < GDN_HEADS, g, _sigmoid(ab))
    emit_plain(len(plain))


def _inproj(x2, g, w_all, conv_w, gparams, tm, tiles_per_seq):
    T = x2.shape[0]
    n_all = w_all.shape[1]
    rows = lambda w: pl.BlockSpec((tm, w), lambda i: (i, 0))
    return pl.pallas_call(
        functools.partial(_inproj_kernel, tiles_per_seq=tiles_per_seq),
        grid=(T // tm,),
        in_specs=[
            rows(D_MODEL),
            _const_spec((1, D_MODEL)),
            _const_spec((D_MODEL, n_all)),
            _const_spec((GDN_CONV, 3 * GDN_QK)),
            _const_spec((8, LANES)),
        ],
        out_specs=[rows(3 * GDN_QK), rows(GDN_V), rows(LANES), rows(3 * DIFF_QK), rows(2 * D_MODEL)],
        out_shape=[
            jax.ShapeDtypeStruct((T, 3 * GDN_QK), BF16),
            jax.ShapeDtypeStruct((T, GDN_V), BF16),
            jax.ShapeDtypeStruct((T, LANES), F32),
            jax.ShapeDtypeStruct((T, 3 * DIFF_QK), BF16),
            jax.ShapeDtypeStruct((T, 2 * D_MODEL), BF16),
        ],
        scratch_shapes=[pltpu.VMEM((tm, D_MODEL), BF16), pltpu.VMEM((_HALO, 3 * GDN_QK), F32)],
        compiler_params=pltpu.CompilerParams(
            dimension_semantics=("arbitrary",), vmem_limit_bytes=VMEM_LIMIT),
        name="inproj",
    )(x2, g, w_all, conv_w, gparams)


def _split3_dot(x, sel):
    hi = x.astype(BF16)
    r1 = x - hi.astype(F32)
    mid = r1.astype(BF16)
    lo = (r1 - mid.astype(F32)).astype(BF16)
    return _dot(hi, sel) + _dot(mid, sel) + _dot(lo, sel)


def _pair(a, b):
    return jnp.concatenate([a, b], axis=1)


def _blockdiag2(x):
    c = x.shape[0]
    z = jnp.zeros((c, c), x.dtype)
    return jnp.concatenate([_pair(x[:, :c], z), _pair(z, x[:, c:])], axis=0)


def _gdn_kernel(q_ref, k_ref, v_ref, z_ref, g_ref, beta_ref, onorm_ref, o_ref,
                m_s, n_s, qe_s, au_s, gl_s, st_s):
    S = q_ref.shape[1]
    C = GDN_CHUNK
    NC = S // C

    ii = lax.broadcasted_iota(jnp.int32, (C, C), 0)
    jj = lax.broadcasted_iota(jnp.int32, (C, C), 1)
    ii2 = _pair(ii, ii)
    jj2 = _pair(jj, jj)
    strict2 = ii2 > jj2
    incl2 = ii2 >= jj2
    eye2 = (ii2 == jj2).astype(F32)

    gc = _split3_dot(g_ref[0, 0], (ii <= jj).astype(BF16))
    beta = beta_ref[0, 0]

    tiles = []
    for c in range(NC):
        gc_row = gc[c:c + 1, :]
        rg = jnp.broadcast_to(gc_row, (C, C))
        cg = rg.T
        glast = cg[C - 1:C, :]
        kf = k_ref[0, c * C:(c + 1) * C, :].astype(F32)
        eg = jnp.exp(cg)
        tiles.append(dict(
            decay=jnp.exp(jnp.minimum(cg - rg, 0.0)),
            beta_row=beta[c:c + 1, :],
            k=k_ref[0, c * C:(c + 1) * C, :],
            q=q_ref[0, c * C:(c + 1) * C, :],
            k_eg=(kf * eg).astype(BF16),
            q_dec=q_ref[0, c * C:(c + 1) * C, :].astype(F32) * eg,
            kd_t=(kf.T * (jnp.exp(glast - gc_row) * beta[c:c + 1, :])).astype(BF16),
            gl=jnp.exp(glast),
        ))

    pairs = [(tiles[c0], tiles[c0 + 1]) for c0 in range(0, NC, 2)]
    b_mats, aqs = [], []
    for ta, tb in pairs:
        k2 = _pair(ta["k"], tb["k"])
        kq = _dot_nt(jnp.concatenate([k2, _pair(ta["q"], tb["q"])], axis=0), _blockdiag2(k2))
        decay2 = _pair(ta["decay"], tb["decay"])
        beta2 = jnp.broadcast_to(_pair(ta["beta_row"], tb["beta_row"]), (C, 2 * C))
        b_mats.append(jnp.where(strict2, kq[0:C, :] * decay2, 0.0) * beta2)
        aqs.append((jnp.where(incl2, kq[C:2 * C, :] * decay2, 0.0) * beta2).astype(BF16))

    ps = [eye2 - jnp.where((ii2 >> 1) == (jj2 >> 1), b, 0.0) for b in b_mats]
    lvl = 1
    while (1 << lvl) < C:
        in_parent = (ii2 >> (lvl + 1)) == (jj2 >> (lvl + 1))
        in_child = (ii2 >> lvl) == (jj2 >> lvl)
        e_mask = jnp.logical_and(in_parent, jnp.logical_not(in_child))
        pbs = [p.astype(BF16) for p in ps]
        pes = [_dot(pb, _blockdiag2(jnp.where(e_mask, b, 0.0).astype(BF16)))
               for pb, b in zip(pbs, b_mats)]
        ps = [p - _dot(pe.astype(BF16), _blockdiag2(pb)) for p, pe, pb in zip(ps, pes, pbs)]
        lvl += 1

    wus = []
    for pi, (p, (ta, tb)) in enumerate(zip(ps, pairs)):
        pb = p.astype(BF16)
        for half, t in enumerate((ta, tb)):
            c = 2 * pi + half
            vc = v_ref[0, c * C:(c + 1) * C, :]
            wus.append(_dot(pb[:, half * C:(half + 1) * C], _pair(t["k_eg"], vc)).astype(BF16))

    for c, (t, wu) in enumerate(zip(tiles, wus)):
        aq_c = aqs[c // 2][:, (c % 2) * C:(c % 2 + 1) * C]
        r2 = _dot(jnp.concatenate([t["kd_t"], aq_c], axis=0), wu)
        m_s[c] = r2[0:C, 0:GDN_DK].astype(BF16)
        n_s[c] = r2[0:C, GDN_DK:GDN_DK + GDN_DV]
        qe_s[c] = (t["q_dec"] - r2[C:2 * C, 0:GDN_DK]).astype(BF16)
        au_s[c] = r2[C:2 * C, GDN_DK:GDN_DK + GDN_DV]
        gl_s[c] = jnp.broadcast_to(t["gl"], (8, C))

    def chunk_step(c, state):
        sb = state.astype(BF16)
        st_s[c] = sb
        return state * gl_s[c][0:1, :] - _dot(m_s[c], sb) + n_s[c]

    lax.fori_loop(0, NC, chunk_step, jnp.zeros((GDN_DK, GDN_DV), F32))

    o = (_bdot(qe_s[...], st_s[...]) + au_s[...]).reshape(S, GDN_DV)
    ms = jnp.mean(o * o, axis=-1, keepdims=True)
    o = o * lax.rsqrt(ms + NORM_EPS) * onorm_ref[...]
    o_ref[0] = (o * z_ref[0].astype(F32)).astype(o_ref.dtype)


def _gdn(qkv3, zs3, gbt, onorm):
    B, S, _ = qkv3.shape
    NC = S // GDN_CHUNK
    H = GDN_HEADS
    C = GDN_CHUNK
    seq_spec = lambda off: pl.BlockSpec((1, S, LANES), lambda b, h, off=off: (b, 0, off + h))
    gate_spec = lambda off: pl.BlockSpec((1, 1, NC, C), lambda b, h, off=off: (b, off + h, 0, 0))
    return pl.pallas_call(
        _gdn_kernel,
        grid=(B, H),
        in_specs=[
            seq_spec(0), seq_spec(H), seq_spec(2 * H),
            pl.BlockSpec((1, S, LANES), lambda b, h: (b, 0, h)),
            gate_spec(0), gate_spec(H),
            pl.BlockSpec((1, LANES), lambda b, h: (0, 0)),
        ],
        out_specs=pl.BlockSpec((1, S, LANES), lambda b, h: (b, 0, h)),
        out_shape=jax.ShapeDtypeStruct((B, S, GDN_V), BF16),
        scratch_shapes=[
            pltpu.VMEM((NC, GDN_DK, C), BF16),
            pltpu.VMEM((NC, GDN_DK, GDN_DV), F32),
            pltpu.VMEM((NC, C, GDN_DK), BF16),
            pltpu.VMEM((NC, C, GDN_DV), F32),
            pltpu.VMEM((NC, 8, LANES), F32),
            pltpu.VMEM((NC, GDN_DK, GDN_DV), BF16),
        ],
        compiler_params=pltpu.CompilerParams(
            dimension_semantics=("arbitrary", "arbitrary"), vmem_limit_bytes=VMEM_LIMIT),
        name="gdn",
    )(qkv3, qkv3, qkv3, zs3, gbt, gbt, onorm)


_ATT_BLOCK = 256


def _diffattn_kernel(lam_ref, subln_ref, q_ref, k_ref, v_ref, o_ref, s_scr, p_scr, vext_scr,
                     *, lambda_init):
    S = q_ref.shape[1]
    tq = min(_ATT_BLOCK, S)
    dv = 2 * DIFF_DH
    neg = jnp.finfo(F32).min

    lp = lam_ref[...]
    lam = (jnp.exp(jnp.sum(lp[0:1, :] * lp[1:2, :], axis=-1, keepdims=True))
           - jnp.exp(jnp.sum(lp[2:3, :] * lp[3:4, :], axis=-1, keepdims=True))
           + lambda_init)

    vext_scr[:, 0:dv] = v_ref[0]
    vext_scr[:, dv:2 * dv] = jnp.ones((S, dv), BF16)

    lane = lax.broadcasted_iota(jnp.int32, (tq, dv), 1)
    causal = (lax.broadcasted_iota(jnp.int32, (2 * tq, tq), 1)
              <= jnp.bitwise_and(lax.broadcasted_iota(jnp.int32, (2 * tq, tq), 0), tq - 1))

    for qb in range(S // tq):
        r0 = qb * tq
        n = r0 + tq
        qs = q_ref[0, r0:n, :] * (DIFF_DH ** -0.5)
        zero = jnp.zeros_like(qs)
        qq = jnp.concatenate([jnp.where(lane < DIFF_DH, qs, zero),
                              jnp.where(lane >= DIFF_DH, qs, zero)], axis=0)

        m_run = None
        for c0 in range(0, n, tq):
            s = _dot_nt(qq, k_ref[0, c0:c0 + tq, :])
            if c0 == r0:
                s = jnp.where(causal, s, neg)
            s_scr[:, c0:c0 + tq] = s
            for t in range(0, tq, LANES):
                tile = s[:, t:t + LANES]
                m_run = tile if m_run is None else jnp.maximum(m_run, tile)
        m_b = jnp.broadcast_to(jnp.max(m_run, axis=-1, keepdims=True), (2 * tq, LANES))

        for c0 in range(0, n, tq):
            for t in range(c0, c0 + tq, LANES):
                p_scr[:, t:t + LANES] = jnp.exp(s_scr[:, t:t + LANES] - m_b).astype(BF16)

        oe = _dot(p_scr[:, 0:n], vext_scr[0:n, :])
        a = (oe[0:tq, 0:dv] / oe[0:tq, dv:2 * dv]
             - lam * (oe[tq:2 * tq, 0:dv] / oe[tq:2 * tq, dv:2 * dv]))
        ms = jnp.mean(a * a, axis=-1, keepdims=True)
        a = a * lax.rsqrt(ms + NORM_EPS) * subln_ref[...] * (1.0 - lambda_init)
        o_ref[0, r0:n, :] = a.astype(o_ref.dtype)


def _diffattn(dqkv3, lam_params, subln, lambda_init):
    B, S, _ = dqkv3.shape
    H = DIFF_HEADS
    tq = min(_ATT_BLOCK, S)
    dv = 2 * DIFF_DH
    head = lambda off: pl.BlockSpec((1, S, dv), lambda b, h, off=off: (b, 0, off + h))
    return pl.pallas_call(
        functools.partial(_diffattn_kernel, lambda_init=lambda_init),
        grid=(B, H),
        in_specs=[
            pl.BlockSpec((8, DIFF_DH), lambda b, h: (0, 0)),
            pl.BlockSpec((1, dv), lambda b, h: (0, 0)),
            head(0), head(H), head(2 * H),
        ],
        out_specs=pl.BlockSpec((1, S, dv), lambda b, h: (b, 0, h)),
        out_shape=jax.ShapeDtypeStruct((B, S, DIFF_V), BF16),
        scratch_shapes=[
            pltpu.VMEM((2 * tq, S), F32),
            pltpu.VMEM((2 * tq, S), BF16),
            pltpu.VMEM((S, 2 * dv), BF16),
        ],
        compiler_params=pltpu.CompilerParams(
            dimension_semantics=("arbitrary", "arbitrary"), vmem_limit_bytes=VMEM_LIMIT),
        name="diffattn",
    )(lam_params, subln, dqkv3, dqkv3, dqkv3)


def _merge_kernel(x_ref, og_ref, od_ref, gates_ref, wbg_ref, wbd_ref, wout_ref,
                  npost_ref, npre_ref, x1_ref, h2_ref):
    mg = _dot(og_ref[...], wbg_ref[...])
    md = _dot(od_ref[...], wbd_ref[...])
    gg = _sigmoid(gates_ref[:, 0:D_MODEL].astype(F32))
    gd = _sigmoid(gates_ref[:, D_MODEL:2 * D_MODEL].astype(F32))
    merged = (gg * mg + gd * md).astype(BF16)
    m = _dot(merged, wout_ref[...])
    ms = jnp.mean(m * m, axis=-1, keepdims=True)
    x1 = x_ref[...] + m * lax.rsqrt(ms + NORM_EPS) * npost_ref[...]
    x1_ref[...] = x1
    ms1 = jnp.mean(x1 * x1, axis=-1, keepdims=True)
    h2_ref[...] = (x1 * lax.rsqrt(ms1 + NORM_EPS) * npre_ref[...]).astype(BF16)


def _merge(x2, og, od, gates, wbg, wbd, wout, npost, npre, tm):
    T = x2.shape[0]
    row = lambda w: pl.BlockSpec((tm, w), lambda i: (i, 0))
    return pl.pallas_call(
        _merge_kernel,
        grid=(T // tm,),
        in_specs=[
            row(D_MODEL), row(GDN_V), row(DIFF_V), row(2 * D_MODEL),
            _const_spec((GDN_V, D_MODEL)), _const_spec((DIFF_V, D_MODEL)),
            _const_spec((D_MODEL, D_MODEL)),
            _const_spec((1, D_MODEL)), _const_spec((1, D_MODEL)),
        ],
        out_specs=[row(D_MODEL), row(D_MODEL)],
        out_shape=[jax.ShapeDtypeStruct((T, D_MODEL), F32),
                   jax.ShapeDtypeStruct((T, D_MODEL), BF16)],
        compiler_params=pltpu.CompilerParams(
            dimension_semantics=("arbitrary",), vmem_limit_bytes=VMEM_LIMIT),
        name="merge",
    )(x2, og, od, gates, wbg, wbd, wout, npost, npre)


_FFN_CHUNK = 512


def _ffn_kernel(x1_ref, h2_ref, wup_ref, cw_ref, cb_ref, wdown_ref, npost_ref, out_ref,
                carry, act_scr):
    si = pl.program_id(1)
    tm = h2_ref.shape[1]

    @pl.when(si == 0)
    def _():
        carry[...] = jnp.zeros_like(carry)

    h2 = h2_ref[0]

    def conv_half(col0, cw):
        u = _dot(h2, wup_ref[:, col0:col0 + cw])
        prev = carry[:, col0:col0 + cw]
        carry[:, col0:col0 + cw] = u[tm - _HALO:tm, :]
        return _causal_conv(u, prev, cw_ref[:, col0:col0 + cw], FFN_CONV) + cb_ref[:, col0:col0 + cw]

    for c0 in range(0, D_FF, _FFN_CHUNK):
        cw = min(_FFN_CHUNK, D_FF - c0)
        gate = conv_half(c0, cw)
        up = conv_half(D_FF + c0, cw)
        act_scr[:, c0:c0 + cw] = (_silu(gate) * up).astype(BF16)

    f = _dot(act_scr[...], wdown_ref[...])
    ms = jnp.mean(f * f, axis=-1, keepdims=True)
    out_ref[0] = x1_ref[0] + f * lax.rsqrt(ms + NORM_EPS) * npost_ref[...]


def _ffn(x1_3, h2_3, wup, cw, cb, wdown, npost, tm):
    B, S, _ = x1_3.shape
    seq = lambda: pl.BlockSpec((1, tm, D_MODEL), lambda b, s: (b, s, 0))
    return pl.pallas_call(
        _ffn_kernel,
        grid=(B, S // tm),
        in_specs=[
            seq(), seq(),
            _const_spec((D_MODEL, 2 * D_FF)),
            _const_spec((FFN_CONV, 2 * D_FF)),
            _const_spec((1, 2 * D_FF)),
            _const_spec((D_FF, D_MODEL)),
            _const_spec((1, D_MODEL)),
        ],
        out_specs=seq(),
        out_shape=jax.ShapeDtypeStruct((B, S, D_MODEL), F32),
        scratch_shapes=[
            pltpu.VMEM((_HALO, 2 * D_FF), F32),
            pltpu.VMEM((tm, D_FF), BF16),
        ],
        compiler_params=pltpu.CompilerParams(
            dimension_semantics=("arbitrary", "arbitrary"), vmem_limit_bytes=VMEM_LIMIT),
        name="convffn",
    )(x1_3, h2_3, wup, cw, cb, wdown, npost)


def _row_tile(n, want):
    t = min(want, n)
    while n % t:
        t //= 2
    return t


def _layer(x, lambda_init, norm_mix_pre, w_in, conv_qkv_w, gdn_A_log, gdn_dt_bias, gdn_out_norm,
           lambda_q1, lambda_k1, lambda_q2, lambda_k2, diff_subln, w_branch_gdn, w_branch_diff,
           w_out, norm_mix_post, norm_ffn_pre, w_up, ffn_conv_w, ffn_conv_b, w_down, norm_ffn_post):
    B, S, D = x.shape
    T = B * S
    x2 = x.reshape(T, D)
    row2 = lambda v: v.reshape(1, -1).astype(F32)

    n_gdn = 4 * GDN_QK
    w_ab = jnp.pad(w_in[:, n_gdn:n_gdn + 2 * GDN_HEADS], ((0, 0), (0, LANES - 2 * GDN_HEADS)))
    w_all = jnp.concatenate([w_in[:, :n_gdn], w_ab, w_in[:, n_gdn + 2 * GDN_HEADS:]],
                            axis=1).astype(BF16)
    gparams = jnp.zeros((8, LANES), F32)
    gparams = gparams.at[0, :GDN_HEADS].set(gdn_A_log.astype(F32))
    gparams = gparams.at[1, :GDN_HEADS].set(gdn_dt_bias.astype(F32))

    tm = _row_tile(S, 512)
    qkv, zs, gb, dqkv, gates = _inproj(x2, row2(norm_mix_pre), w_all, conv_qkv_w.astype(F32),
                                       gparams, tm, S // tm)

    gbt = jnp.transpose(gb.reshape(B, S, LANES)[:, :, :2 * GDN_HEADS], (0, 2, 1))
    gbt = gbt.reshape(B, 2 * GDN_HEADS, S // GDN_CHUNK, GDN_CHUNK)
    o_gdn = _gdn(qkv.reshape(B, S, -1), zs.reshape(B, S, -1), gbt, row2(gdn_out_norm))

    lam_params = jnp.zeros((8, DIFF_DH), F32)
    lam_params = lam_params.at[0].set(lambda_q1.astype(F32)).at[1].set(lambda_k1.astype(F32))
    lam_params = lam_params.at[2].set(lambda_q2.astype(F32)).at[3].set(lambda_k2.astype(F32))
    o_diff = _diffattn(dqkv.reshape(B, S, -1), lam_params, row2(diff_subln), lambda_init)

    x1, h2 = _merge(x2, o_gdn.reshape(T, -1), o_diff.reshape(T, -1), gates,
                    w_branch_gdn.astype(BF16), w_branch_diff.astype(BF16), w_out.astype(BF16),
                    row2(norm_mix_post), row2(norm_ffn_pre), _row_tile(T, 512))

    out = _ffn(x1.reshape(B, S, D), h2.reshape(B, S, D), w_up.astype(BF16),
               ffn_conv_w.astype(F32), row2(ffn_conv_b), w_down.astype(BF16),
               row2(norm_ffn_post), _row_tile(S, 512))
    return out


def kernel(x, norm_mix_pre, w_in, conv_qkv_w, gdn_A_log, gdn_dt_bias, gdn_out_norm, lambda_q1, lambda_k1, lambda_q2, lambda_k2, diff_subln, w_branch_gdn, w_branch_diff, w_out, norm_mix_post, norm_ffn_pre, w_up, ffn_conv_w, ffn_conv_b, w_down, norm_ffn_post):
    depth = w_in.shape[0]
    for l in range(depth):
        lambda_init = 0.8 - 0.6 * math.exp(-0.3 * l)
        x = _layer(x, lambda_init, norm_mix_pre[l], w_in[l], conv_qkv_w[l], gdn_A_log[l],
                   gdn_dt_bias[l], gdn_out_norm[l], lambda_q1[l], lambda_k1[l], lambda_q2[l],
                   lambda_k2[l], diff_subln[l], w_branch_gdn[l], w_branch_diff[l], w_out[l],
                   norm_mix_post[l], norm_ffn_pre[l], w_up[l], ffn_conv_w[l], ffn_conv_b[l],
                   w_down[l], norm_ffn_post[l])
    return x
```

```python
import functools
import math

import jax
import jax.numpy as jnp
from jax import lax
from jax.experimental import pallas as pl
from jax.experimental.pallas import tpu as pltpu

F32 = jnp.float32
BF16 = jnp.bfloat16

D_MODEL = 1024
GDN_HEADS = 8
GDN_DK = 128
GDN_DV = 128
GDN_CONV = 4
DIFF_HEADS = 8
DIFF_DH = 64
D_FF = 2816
FFN_CONV = 3
NORM_EPS = 1e-6

GDN_QK = GDN_HEADS * GDN_DK
GDN_V = GDN_HEADS * GDN_DV
DIFF_QK = DIFF_HEADS * 2 * DIFF_DH
DIFF_V = DIFF_HEADS * 2 * DIFF_DH

LANES = 128
GDN_CHUNK = 128
VMEM_LIMIT = 56 * 1024 * 1024


def _dot(a, b):
    return jnp.dot(a, b, preferred_element_type=F32)


def _dot_nt(a, b):
    return lax.dot_general(a, b, (((1,), (1,)), ((), ())), preferred_element_type=F32)


def _bdot(a, b):
    return lax.dot_general(a, b, (((2,), (1,)), ((0,), (0,))), preferred_element_type=F32)


def _sigmoid(x):
    return 1.0 / (1.0 + jnp.exp(-x))


def _silu(x):
    return x * _sigmoid(x)


def _softplus(x):
    return jnp.maximum(x, 0.0) + jnp.log1p(jnp.exp(-jnp.abs(x)))


def _const_spec(shape):
    nd = len(shape)
    return pl.BlockSpec(shape, lambda *_: (0,) * nd, pipeline_mode=pl.Buffered(1))


_HALO = 8


def _causal_conv(u, prev, w, taps):
    row = lax.broadcasted_iota(jnp.int32, prev.shape, 0)
    y = u * w[taps - 1:taps, :]
    for sh in range(1, taps):
        r = pltpu.roll(u, sh, 0)
        head = jnp.where(row < sh, pltpu.roll(prev, sh, 0), r[0:_HALO, :])
        r = jnp.concatenate([head, r[_HALO:, :]], axis=0)
        y = y + r * w[taps - 1 - sh:taps - sh, :]
    return y


_IN_CHUNK = 512


def _inproj_kernel(x_ref, g_ref, w_ref, cw_ref, gp_ref, qkv_ref, z_ref, gb_ref, dqkv_ref, gates_ref,
                   h_scr, carry, *, tiles_per_seq):
    tm = x_ref.shape[0]

    @pl.when(pl.program_id(0) % tiles_per_seq == 0)
    def _():
        carry[...] = jnp.zeros_like(carry)

    x = x_ref[...]
    ms = jnp.mean(x * x, axis=-1, keepdims=True)
    h_scr[...] = (x * lax.rsqrt(ms + NORM_EPS) * g_ref[...]).astype(BF16)

    w_plain = 4 * GDN_QK + LANES
    plain = ([(dqkv_ref, c0, w_plain + c0) for c0 in range(0, 3 * DIFF_QK, _IN_CHUNK)]
             + [(gates_ref, c0, w_plain + 3 * DIFF_QK + c0) for c0 in range(0, 2 * D_MODEL, _IN_CHUNK)])

    def emit_plain(count):
        for _ in range(min(count, len(plain))):
            out_ref, c0, w_off = plain.pop(0)
            acc = _dot(h_scr[...], w_ref[:, w_off:w_off + _IN_CHUNK])
            out_ref[:, c0:c0 + _IN_CHUNK] = acc.astype(out_ref.dtype)

    def gdn_chunk(c0, u):
        prev = carry[:, c0:c0 + _IN_CHUNK]
        carry[:, c0:c0 + _IN_CHUNK] = u[tm - _HALO:tm, :]
        y = _silu(_causal_conv(u, prev, cw_ref[:, c0:c0 + _IN_CHUNK], GDN_CONV))
        if c0 < 2 * GDN_QK:
            scale = GDN_DK ** -0.5 if c0 < GDN_QK else 1.0
            heads = []
            for h0 in range(0, _IN_CHUNK, GDN_DK):
                yh = y[:, h0:h0 + GDN_DK]
                inv = lax.rsqrt(jnp.sum(yh * yh, axis=-1, keepdims=True) + NORM_EPS)
                heads.append(yh * (inv * scale))
            y = jnp.concatenate(heads, axis=1)
        qkv_ref[:, c0:c0 + _IN_CHUNK] = y.astype(BF16)

    pending = None
    for c0 in range(0, 3 * GDN_QK, _IN_CHUNK):
        u = _dot(h_scr[...], w_ref[:, c0:c0 + _IN_CHUNK])
        emit_plain(1)
        if pending is not None:
            gdn_chunk(*pending)
        pending = (c0, u)
    gdn_chunk(*pending)

    w_off = 3 * GDN_QK
    for c0 in range(0, GDN_V, _IN_CHUNK):
        emit_plain(1)
        zc = _dot(h_scr[...], w_ref[:, w_off + c0:w_off + c0 + _IN_CHUNK])
        z_ref[:, c0:c0 + _IN_CHUNK] = _silu(zc).astype(BF16)

    w_off += GDN_V
    ab = _dot(h_scr[...], w_ref[:, w_off:w_off + LANES])
    gp = gp_ref[...]
    g = -jnp.exp(gp[0:1, :]) * _softplus(ab + gp[1:2, :])
    lane = lax.broadcasted_iota(jnp.int32, ab.shape, 1)
    gb_ref[...] = jnp.where(lane < GDN_HEADS, g, _sigmoid(ab))
    emit_plain(len(plain))


def _inproj(x2, g, w_all, conv_w, gparams, tm, tiles_per_seq):
    T = x2.shape[0]
    n_all = w_all.shape[1]
    rows = lambda w: pl.BlockSpec((tm, w), lambda i: (i, 0))
    return pl.pallas_call(
        functools.partial(_inproj_kernel, tiles_per_seq=tiles_per_seq),
        grid=(T // tm,),
        in_specs=[
            rows(D_MODEL),
            _const_spec((1, D_MODEL)),
            _const_spec((D_MODEL, n_all)),
            _const_spec((GDN_CONV, 3 * GDN_QK)),
            _const_spec((8, LANES)),
        ],
        out_specs=[rows(3 * GDN_QK), rows(GDN_V), rows(LANES), rows(3 * DIFF_QK), rows(2 * D_MODEL)],
        out_shape=[
            jax.ShapeDtypeStruct((T, 3 * GDN_QK), BF16),
            jax.ShapeDtypeStruct((T, GDN_V), BF16),
            jax.ShapeDtypeStruct((T, LANES), F32),
            jax.ShapeDtypeStruct((T, 3 * DIFF_QK), BF16),
            jax.ShapeDtypeStruct((T, 2 * D_MODEL), BF16),
        ],
        scratch_shapes=[pltpu.VMEM((tm, D_MODEL), BF16), pltpu.VMEM((_HALO, 3 * GDN_QK), F32)],
        compiler_params=pltpu.CompilerParams(
            dimension_semantics=("arbitrary",), vmem_limit_bytes=VMEM_LIMIT),
        name="inproj",
    )(x2, g, w_all, conv_w, gparams)


def _split3_dot(x, sel):
    hi = x.astype(BF16)
    r1 = x - hi.astype(F32)
    mid = r1.astype(BF16)
    lo = (r1 - mid.astype(F32)).astype(BF16)
    return _dot(hi, sel) + _dot(mid, sel) + _dot(lo, sel)


def _pair(a, b):
    return jnp.concatenate([a, b], axis=1)


def _blockdiag2(x):
    c = x.shape[0]
    z = jnp.zeros((c, c), x.dtype)
    return jnp.concatenate([_pair(x[:, :c], z), _pair(z, x[:, c:])], axis=0)


def _gdn_kernel(q_ref, k_ref, v_ref, z_ref, g_ref, beta_ref, onorm_ref, o_ref,
                m_s, n_s, qe_s, au_s, gl_s, st_s):
    S = q_ref.shape[1]
    C = GDN_CHUNK
    NC = S // C

    ii = lax.broadcasted_iota(jnp.int32, (C, C), 0)
    jj = lax.broadcasted_iota(jnp.int32, (C, C), 1)
    ii2 = _pair(ii, ii)
    jj2 = _pair(jj, jj)
    strict2 = ii2 > jj2
    incl2 = ii2 >= jj2
    eye2 = (ii2 == jj2).astype(F32)

    gc = _split3_dot(g_ref[0, 0], (ii <= jj).astype(BF16))
    beta = beta_ref[0, 0]

    tiles = []
    for c in range(NC):
        gc_row = gc[c:c + 1, :]
        rg = jnp.broadcast_to(gc_row, (C, C))
        cg = rg.T
        glast = cg[C - 1:C, :]
        kf = k_ref[0, c * C:(c + 1) * C, :].astype(F32)
        eg = jnp.exp(cg)
        tiles.append(dict(
            decay=jnp.exp(jnp.minimum(cg - rg, 0.0)),
            beta_row=beta[c:c + 1, :],
            k=k_ref[0, c * C:(c + 1) * C, :],
            q=q_ref[0, c * C:(c + 1) * C, :],
            k_eg=(kf * eg).astype(BF16),
            q_dec=q_ref[0, c * C:(c + 1) * C, :].astype(F32) * eg,
            kd_t=(kf.T * (jnp.exp(glast - gc_row) * beta[c:c + 1, :])).astype(BF16),
            gl=jnp.exp(glast),
        ))

    pairs = [(tiles[c0], tiles[c0 + 1]) for c0 in range(0, NC, 2)]
    b_mats, aqs = [], []
    for ta, tb in pairs:
        k2 = _pair(ta["k"], tb["k"])
        kq = _dot_nt(jnp.concatenate([k2, _pair(ta["q"], tb["q"])], axis=0), _blockdiag2(k2))
        decay2 = _pair(ta["decay"], tb["decay"])
        beta2 = jnp.broadcast_to(_pair(ta["beta_row"], tb["beta_row"]), (C, 2 * C))
        b_mats.append(jnp.where(strict2, kq[0:C, :] * decay2, 0.0) * beta2)
        aqs.append((jnp.where(incl2, kq[C:2 * C, :] * decay2, 0.0) * beta2).astype(BF16))

    ps = [eye2 - jnp.where((ii2 >> 1) == (jj2 >> 1), b, 0.0) for b in b_mats]
    lvl = 1
    while (1 << lvl) < C:
        in_parent = (ii2 >> (lvl + 1)) == (jj2 >> (lvl + 1))
        in_child = (ii2 >> lvl) == (jj2 >> lvl)
        e_mask = jnp.logical_and(in_parent, jnp.logical_not(in_child))
        pbs = [p.astype(BF16) for p in ps]
        pes = [_dot(pb, _blockdiag2(jnp.where(e_mask, b, 0.0).astype(BF16)))
               for pb, b in zip(pbs, b_mats)]
        ps = [p - _dot(pe.astype(BF16), _blockdiag2(pb)) for p, pe, pb in zip(ps, pes, pbs)]
        lvl += 1

    wus = []
    for pi, (p, (ta, tb)) in enumerate(zip(ps, pairs)):
        pb = p.astype(BF16)
        for half, t in enumerate((ta, tb)):
            c = 2 * pi + half
            vc = v_ref[0, c * C:(c + 1) * C, :]
            wus.append(_dot(pb[:, half * C:(half + 1) * C], _pair(t["k_eg"], vc)).astype(BF16))

    for c, (t, wu) in enumerate(zip(tiles, wus)):
        aq_c = aqs[c // 2][:, (c % 2) * C:(c % 2 + 1) * C]
        r2 = _dot(jnp.concatenate([t["kd_t"], aq_c], axis=0), wu)
        m_s[c] = r2[0:C, 0:GDN_DK].astype(BF16)
        n_s[c] = r2[0:C, GDN_DK:GDN_DK + GDN_DV]
        qe_s[c] = (t["q_dec"] - r2[C:2 * C, 0:GDN_DK]).astype(BF16)
        au_s[c] = r2[C:2 * C, GDN_DK:GDN_DK + GDN_DV]
        gl_s[c] = jnp.broadcast_to(t["gl"], (8, C))

    def chunk_step(c, state):
        sb = state.astype(BF16)
        st_s[c] = sb
        return state * gl_s[c][0:1, :] - _dot(m_s[c], sb) + n_s[c]

    lax.fori_loop(0, NC, chunk_step, jnp.zeros((GDN_DK, GDN_DV), F32))

    o = (_bdot(qe_s[...], st_s[...]) + au_s[...]).reshape(S, GDN_DV)
    ms = jnp.mean(o * o, axis=-1, keepdims=True)
    o = o * lax.rsqrt(ms + NORM_EPS) * onorm_ref[...]
    o_ref[0] = (o * z_ref[0].astype(F32)).astype(o_ref.dtype)


def _gdn(qkv3, zs3, gbt, onorm):
    B, S, _ = qkv3.shape
    NC = S // GDN_CHUNK
    H = GDN_HEADS
    C = GDN_CHUNK
    seq_spec = lambda off: pl.BlockSpec((1, S, LANES), lambda b, h, off=off: (b, 0, off + h))
    gate_spec = lambda off: pl.BlockSpec((1, 1, NC, C), lambda b, h, off=off: (b, off + h, 0, 0))
    return pl.pallas_call(
        _gdn_kernel,
        grid=(B, H),
        in_specs=[
            seq_spec(0), seq_spec(H), seq_spec(2 * H),
            pl.BlockSpec((1, S, LANES), lambda b, h: (b, 0, h)),
            gate_spec(0), gate_spec(H),
            pl.BlockSpec((1, LANES), lambda b, h: (0, 0)),
        ],
        out_specs=pl.BlockSpec((1, S, LANES), lambda b, h: (b, 0, h)),
        out_shape=jax.ShapeDtypeStruct((B, S, GDN_V), BF16),
        scratch_shapes=[
            pltpu.VMEM((NC, GDN_DK, C), BF16),
            pltpu.VMEM((NC, GDN_DK, GDN_DV), F32),
            pltpu.VMEM((NC, C, GDN_DK), BF16),
            pltpu.VMEM((NC, C, GDN_DV), F32),
            pltpu.VMEM((NC, 8, LANES), F32),
            pltpu.VMEM((NC, GDN_DK, GDN_DV), BF16),
        ],
        compiler_params=pltpu.CompilerParams(
            dimension_semantics=("arbitrary", "arbitrary"), vmem_limit_bytes=VMEM_LIMIT),
        name="gdn",
    )(qkv3, qkv3, qkv3, zs3, gbt, gbt, onorm)


_ATT_BLOCK = 256
_LOG2E = math.log2(math.e)


def _diffattn_kernel(lam_ref, subln_ref, q_ref, k_ref, v_ref, o_ref, s_scr, p_scr, vext_scr,
                     *, lambda_init):
    S = q_ref.shape[1]
    tq = min(_ATT_BLOCK, S)
    dv = 2 * DIFF_DH
    neg = jnp.finfo(F32).min

    lp = lam_ref[...]
    lam = (jnp.exp(jnp.sum(lp[0:1, :] * lp[1:2, :], axis=-1, keepdims=True))
           - jnp.exp(jnp.sum(lp[2:3, :] * lp[3:4, :], axis=-1, keepdims=True))
           + lambda_init)

    vext_scr[:, 0:dv] = v_ref[0]
    vext_scr[:, dv:2 * dv] = jnp.ones((S, dv), BF16)

    lane = lax.broadcasted_iota(jnp.int32, (tq, dv), 1)
    causal = (lax.broadcasted_iota(jnp.int32, (2 * tq, tq), 1)
              <= jnp.bitwise_and(lax.broadcasted_iota(jnp.int32, (2 * tq, tq), 0), tq - 1))

    for qb in reversed(range(S // tq)):
        r0 = qb * tq
        n = r0 + tq
        s_buf = s_scr.at[qb % 2]
        p_buf = p_scr.at[qb % 2]
        qs = (q_ref[0, r0:n, :].astype(F32) * (DIFF_DH ** -0.5 * _LOG2E)).astype(BF16)
        zero = jnp.zeros_like(qs)
        qq = jnp.concatenate([jnp.where(lane < DIFF_DH, qs, zero),
                              jnp.where(lane >= DIFF_DH, qs, zero)], axis=0)

        m_run = None
        for c0 in range(0, n, tq):
            s = _dot_nt(qq, k_ref[0, c0:c0 + tq, :])
            if c0 == r0:
                s = jnp.where(causal, s, neg)
            s_buf[c0 // tq] = s
            for t in range(0, tq, LANES):
                tile = s[:, t:t + LANES]
                m_run = tile if m_run is None else jnp.maximum(m_run, tile)
        m_b = jnp.broadcast_to(jnp.max(m_run, axis=-1, keepdims=True), (2 * tq, LANES))

        for c0 in range(0, n, tq):
            for t in range(0, tq, LANES):
                p_buf[:, c0 + t:c0 + t + LANES] = jnp.exp2(
                    s_buf[c0 // tq, :, t:t + LANES] - m_b).astype(BF16)

        oe = _dot(p_buf[:, 0:n], vext_scr[0:n, :])
        a = (oe[0:tq, 0:dv] / oe[0:tq, dv:2 * dv]
             - lam * (oe[tq:2 * tq, 0:dv] / oe[tq:2 * tq, dv:2 * dv]))
        ms = jnp.mean(a * a, axis=-1, keepdims=True)
        a = a * lax.rsqrt(ms + NORM_EPS) * subln_ref[...] * (1.0 - lambda_init)
        o_ref[0, r0:n, :] = a.astype(o_ref.dtype)


def _diffattn(dqkv3, lam_params, subln, lambda_init):
    B, S, _ = dqkv3.shape
    H = DIFF_HEADS
    tq = min(_ATT_BLOCK, S)
    dv = 2 * DIFF_DH
    head = lambda off: pl.BlockSpec((1, S, dv), lambda b, h, off=off: (b, 0, off + h))
    return pl.pallas_call(
        functools.partial(_diffattn_kernel, lambda_init=lambda_init),
        grid=(B, H),
        in_specs=[
            pl.BlockSpec((8, DIFF_DH), lambda b, h: (0, 0)),
            pl.BlockSpec((1, dv), lambda b, h: (0, 0)),
            head(0), head(H), head(2 * H),
        ],
        out_specs=pl.BlockSpec((1, S, dv), lambda b, h: (b, 0, h)),
        out_shape=jax.ShapeDtypeStruct((B, S, DIFF_V), BF16),
        scratch_shapes=[
            pltpu.VMEM((2, S // tq, 2 * tq, tq), F32),
            pltpu.VMEM((2, 2 * tq, S), BF16),
            pltpu.VMEM((S, 2 * dv), BF16),
        ],
        compiler_params=pltpu.CompilerParams(
            dimension_semantics=("arbitrary", "arbitrary"), vmem_limit_bytes=VMEM_LIMIT),
        name="diffattn",
    )(lam_params, subln, dqkv3, dqkv3, dqkv3)


def _merge_kernel(x_ref, og_ref, od_ref, gates_ref, wbg_ref, wbd_ref, wout_ref,
                  npost_ref, npre_ref, x1_ref, h2_ref):
    mg = _dot(og_ref[...], wbg_ref[...])
    md = _dot(od_ref[...], wbd_ref[...])
    gg = _sigmoid(gates_ref[:, 0:D_MODEL].astype(F32))
    gd = _sigmoid(gates_ref[:, D_MODEL:2 * D_MODEL].astype(F32))
    merged = (gg * mg + gd * md).astype(BF16)
    m = _dot(merged, wout_ref[...])
    ms = jnp.mean(m * m, axis=-1, keepdims=True)
    x1 = x_ref[...] + m * lax.rsqrt(ms + NORM_EPS) * npost_ref[...]
    x1_ref[...] = x1
    ms1 = jnp.mean(x1 * x1, axis=-1, keepdims=True)
    h2_ref[...] = (x1 * lax.rsqrt(ms1 + NORM_EPS) * npre_ref[...]).astype(BF16)


def _merge(x2, og, od, gates, wbg, wbd, wout, npost, npre, tm):
    T = x2.shape[0]
    row = lambda w: pl.BlockSpec((tm, w), lambda i: (i, 0))
    return pl.pallas_call(
        _merge_kernel,
        grid=(T // tm,),
        in_specs=[
            row(D_MODEL), row(GDN_V), row(DIFF_V), row(2 * D_MODEL),
            _const_spec((GDN_V, D_MODEL)), _const_spec((DIFF_V, D_MODEL)),
            _const_spec((D_MODEL, D_MODEL)),
            _const_spec((1, D_MODEL)), _const_spec((1, D_MODEL)),
        ],
        out_specs=[row(D_MODEL), row(D_MODEL)],
        out_shape=[jax.ShapeDtypeStruct((T, D_MODEL), F32),
                   jax.ShapeDtypeStruct((T, D_MODEL), BF16)],
        compiler_params=pltpu.CompilerParams(
            dimension_semantics=("arbitrary",), vmem_limit_bytes=VMEM_LIMIT),
        name="merge",
    )(x2, og, od, gates, wbg, wbd, wout, npost, npre)


_FFN_CHUNK = 512


def _ffn_kernel(x1_ref, h2_ref, wup_ref, cw_ref, cb_ref, wdown_ref, npost_ref, out_ref,
                carry, act_scr):
    si = pl.program_id(1)
    tm = h2_ref.shape[1]

    @pl.when(si == 0)
    def _():
        carry[...] = jnp.zeros_like(carry)

    h2 = h2_ref[0]

    def conv_half(col0, cw):
        u = _dot(h2, wup_ref[:, col0:col0 + cw])
        prev = carry[:, col0:col0 + cw]
        carry[:, col0:col0 + cw] = u[tm - _HALO:tm, :]
        return _causal_conv(u, prev, cw_ref[:, col0:col0 + cw], FFN_CONV) + cb_ref[:, col0:col0 + cw]

    for c0 in range(0, D_FF, _FFN_CHUNK):
        cw = min(_FFN_CHUNK, D_FF - c0)
        gate = conv_half(c0, cw)
        up = conv_half(D_FF + c0, cw)
        act_scr[:, c0:c0 + cw] = (_silu(gate) * up).astype(BF16)

    f = _dot(act_scr[...], wdown_ref[...])
    ms = jnp.mean(f * f, axis=-1, keepdims=True)
    out_ref[0] = x1_ref[0] + f * lax.rsqrt(ms + NORM_EPS) * npost_ref[...]


def _ffn(x1_3, h2_3, wup, cw, cb, wdown, npost, tm):
    B, S, _ = x1_3.shape
    seq = lambda: pl.BlockSpec((1, tm, D_MODEL), lambda b, s: (b, s, 0))
    return pl.pallas_call(
        _ffn_kernel,
        grid=(B, S // tm),
        in_specs=[
            seq(), seq(),
            _const_spec((D_MODEL, 2 * D_FF)),
            _const_spec((FFN_CONV, 2 * D_FF)),
            _const_spec((1, 2 * D_FF)),
            _const_spec((D_FF, D_MODEL)),
            _const_spec((1, D_MODEL)),
        ],
        out_specs=seq(),
        out_shape=jax.ShapeDtypeStruct((B, S, D_MODEL), F32),
        scratch_shapes=[
            pltpu.VMEM((_HALO, 2 * D_FF), F32),
            pltpu.VMEM((tm, D_FF), BF16),
        ],
        compiler_params=pltpu.CompilerParams(
            dimension_semantics=("arbitrary", "arbitrary"), vmem_limit_bytes=VMEM_LIMIT),
        name="convffn",
    )(x1_3, h2_3, wup, cw, cb, wdown, npost)


def _row_tile(n, want):
    t = min(want, n)
    while n % t:
        t //= 2
    return t


def _layer(x, lambda_init, norm_mix_pre, w_in, conv_qkv_w, gdn_A_log, gdn_dt_bias, gdn_out_norm,
           lambda_q1, lambda_k1, lambda_q2, lambda_k2, diff_subln, w_branch_gdn, w_branch_diff,
           w_out, norm_mix_post, norm_ffn_pre, w_up, ffn_conv_w, ffn_conv_b, w_down, norm_ffn_post):
    B, S, D = x.shape
    T = B * S
    x2 = x.reshape(T, D)
    row2 = lambda v: v.reshape(1, -1).astype(F32)

    n_gdn = 4 * GDN_QK
    w_ab = jnp.pad(w_in[:, n_gdn:n_gdn + 2 * GDN_HEADS], ((0, 0), (0, LANES - 2 * GDN_HEADS)))
    w_all = jnp.concatenate([w_in[:, :n_gdn], w_ab, w_in[:, n_gdn + 2 * GDN_HEADS:]],
                            axis=1).astype(BF16)
    gparams = jnp.zeros((8, LANES), F32)
    gparams = gparams.at[0, :GDN_HEADS].set(gdn_A_log.astype(F32))
    gparams = gparams.at[1, :GDN_HEADS].set(gdn_dt_bias.astype(F32))

    tm = _row_tile(S, 512)
    qkv, zs, gb, dqkv, gates = _inproj(x2, row2(norm_mix_pre), w_all, conv_qkv_w.astype(F32),
                                       gparams, tm, S // tm)

    gbt = jnp.transpose(gb.reshape(B, S, LANES)[:, :, :2 * GDN_HEADS], (0, 2, 1))
    gbt = gbt.reshape(B, 2 * GDN_HEADS, S // GDN_CHUNK, GDN_CHUNK)
    o_gdn = _gdn(qkv.reshape(B, S, -1), zs.reshape(B, S, -1), gbt, row2(gdn_out_norm))

    lam_params = jnp.zeros((8, DIFF_DH), F32)
    lam_params = lam_params.at[0].set(lambda_q1.astype(F32)).at[1].set(lambda_k1.astype(F32))
    lam_params = lam_params.at[2].set(lambda_q2.astype(F32)).at[3].set(lambda_k2.astype(F32))
    o_diff = _diffattn(dqkv.reshape(B, S, -1), lam_params, row2(diff_subln), lambda_init)

    x1, h2 = _merge(x2, o_gdn.reshape(T, -1), o_diff.reshape(T, -1), gates,
                    w_branch_gdn.astype(BF16), w_branch_diff.astype(BF16), w_out.astype(BF16),
                    row2(norm_mix_post), row2(norm_ffn_pre), _row_tile(T, 512))

    out = _ffn(x1.reshape(B, S, D), h2.reshape(B, S, D), w_up.astype(BF16),
               ffn_conv_w.astype(F32), row2(ffn_conv_b), w_down.astype(BF16),
               row2(norm_ffn_post), _row_tile(S, 512))
    return out


def kernel(x, norm_mix_pre, w_in, conv_qkv_w, gdn_A_log, gdn_dt_bias, gdn_out_norm, lambda_q1, lambda_k1, lambda_q2, lambda_k2, diff_subln, w_branch_gdn, w_branch_diff, w_out, norm_mix_post, norm_ffn_pre, w_up, ffn_conv_w, ffn_conv_b, w_down, norm_ffn_post):
    depth = w_in.shape[0]
    for l in range(depth):
        lambda_init = 0.8 - 0.6 * math.exp(-0.3 * l)
        x = _layer(x, lambda_init, norm_mix_pre[l], w_in[l], conv_qkv_w[l], gdn_A_log[l],
                   gdn_dt_bias[l], gdn_out_norm[l], lambda_q1[l], lambda_k1[l], lambda_q2[l],
                   lambda_k2[l], diff_subln[l], w_branch_gdn[l], w_branch_diff[l], w_out[l],
                   norm_mix_post[l], norm_ffn_pre[l], w_up[l], ffn_conv_w[l], ffn_conv_b[l],
                   w_down[l], norm_ffn_post[l])
    return x
```

```python
import functools
import math

import jax
import jax.numpy as jnp
from jax import lax
from jax.experimental import pallas as pl
from jax.experimental.pallas import tpu as pltpu

F32 = jnp.float32
BF16 = jnp.bfloat16

D_MODEL = 1024
GDN_HEADS = 8
GDN_DK = 128
GDN_DV = 128
GDN_CONV = 4
DIFF_HEADS = 8
DIFF_DH = 64
D_FF = 2816
FFN_CONV = 3
NORM_EPS = 1e-6

GDN_QK = GDN_HEADS * GDN_DK
GDN_V = GDN_HEADS * GDN_DV
DIFF_QK = DIFF_HEADS * 2 * DIFF_DH
DIFF_V = DIFF_HEADS * 2 * DIFF_DH

LANES = 128
GDN_CHUNK = 128
VMEM_LIMIT = 56 * 1024 * 1024


def _dot(a, b):
    return jnp.dot(a, b, preferred_element_type=F32)


def _dot_nt(a, b):
    return lax.dot_general(a, b, (((1,), (1,)), ((), ())), preferred_element_type=F32)


def _bdot(a, b):
    return lax.dot_general(a, b, (((2,), (1,)), ((0,), (0,))), preferred_element_type=F32)


def _sigmoid(x):
    return 1.0 / (1.0 + jnp.exp(-x))


def _silu(x):
    return x * _sigmoid(x)


def _softplus(x):
    return jnp.maximum(x, 0.0) + jnp.log1p(jnp.exp(-jnp.abs(x)))


def _const_spec(shape):
    nd = len(shape)
    return pl.BlockSpec(shape, lambda *_: (0,) * nd, pipeline_mode=pl.Buffered(1))


_HALO = 8


def _causal_conv(u, prev, w, taps):
    row = lax.broadcasted_iota(jnp.int32, prev.shape, 0)
    y = u * w[taps - 1:taps, :]
    for sh in range(1, taps):
        r = pltpu.roll(u, sh, 0)
        head = jnp.where(row < sh, pltpu.roll(prev, sh, 0), r[0:_HALO, :])
        r = jnp.concatenate([head, r[_HALO:, :]], axis=0)
        y = y + r * w[taps - 1 - sh:taps - sh, :]
    return y


_IN_CHUNK = 512


def _inproj_kernel(x_ref, g_ref, wg_ref, wab_ref, wd_ref, cw_ref, gp_ref,
                   qkv_ref, z_ref, gb_ref, dqkv_ref, gates_ref, h_scr, carry, *, tiles_per_seq):
    tm = x_ref.shape[0]

    @pl.when(pl.program_id(0) % tiles_per_seq == 0)
    def _():
        carry[...] = jnp.zeros_like(carry)

    x = x_ref[...]
    ms = jnp.mean(x * x, axis=-1, keepdims=True)
    h_scr[...] = (x * lax.rsqrt(ms + NORM_EPS) * g_ref[...]).astype(BF16)

    plain = ([(dqkv_ref, c0, c0) for c0 in range(0, 3 * DIFF_QK, _IN_CHUNK)]
             + [(gates_ref, c0, 3 * DIFF_QK + c0) for c0 in range(0, 2 * D_MODEL, _IN_CHUNK)])

    def emit_plain(count):
        for _ in range(min(count, len(plain))):
            out_ref, c0, w_off = plain.pop(0)
            acc = _dot(h_scr[...], wd_ref[:, w_off:w_off + _IN_CHUNK])
            out_ref[:, c0:c0 + _IN_CHUNK] = acc.astype(out_ref.dtype)

    for c0 in range(0, 3 * GDN_QK, _IN_CHUNK):
        emit_plain(1)
        u = _dot(h_scr[...], wg_ref[:, c0:c0 + _IN_CHUNK])
        prev = carry[:, c0:c0 + _IN_CHUNK]
        carry[:, c0:c0 + _IN_CHUNK] = u[tm - _HALO:tm, :]
        y = _silu(_causal_conv(u, prev, cw_ref[:, c0:c0 + _IN_CHUNK], GDN_CONV))
        if c0 < 2 * GDN_QK:
            scale = GDN_DK ** -0.5 if c0 < GDN_QK else 1.0
            heads = []
            for h0 in range(0, _IN_CHUNK, GDN_DK):
                yh = y[:, h0:h0 + GDN_DK]
                inv = lax.rsqrt(jnp.sum(yh * yh, axis=-1, keepdims=True) + NORM_EPS)
                heads.append(yh * (inv * scale))
            y = jnp.concatenate(heads, axis=1)
        qkv_ref[:, c0:c0 + _IN_CHUNK] = y.astype(BF16)

    for c0 in range(0, GDN_V, _IN_CHUNK):
        emit_plain(1)
        zc = _dot(h_scr[...], wg_ref[:, 3 * GDN_QK + c0:3 * GDN_QK + c0 + _IN_CHUNK])
        z_ref[:, c0:c0 + _IN_CHUNK] = _silu(zc).astype(BF16)

    ab = _dot(h_scr[...], wab_ref[...])
    gp = gp_ref[...]
    g = -jnp.exp(gp[0:1, :]) * _softplus(ab + gp[1:2, :])
    lane = lax.broadcasted_iota(jnp.int32, ab.shape, 1)
    gb_ref[...] = jnp.where(lane < GDN_HEADS, g, _sigmoid(ab))
    emit_plain(len(plain))


def _inproj(x2, g, w_gdn, w_ab, w_diff, conv_w, gparams, tm, tiles_per_seq):
    T = x2.shape[0]
    rows = lambda w: pl.BlockSpec((tm, w), lambda i: (i, 0))
    return pl.pallas_call(
        functools.partial(_inproj_kernel, tiles_per_seq=tiles_per_seq),
        grid=(T // tm,),
        in_specs=[
            rows(D_MODEL),
            _const_spec((1, D_MODEL)),
            _const_spec(w_gdn.shape),
            _const_spec(w_ab.shape),
            _const_spec(w_diff.shape),
            _const_spec((GDN_CONV, 3 * GDN_QK)),
            _const_spec((8, LANES)),
        ],
        out_specs=[rows(3 * GDN_QK), rows(GDN_V), rows(LANES), rows(3 * DIFF_QK), rows(2 * D_MODEL)],
        out_shape=[
            jax.ShapeDtypeStruct((T, 3 * GDN_QK), BF16),
            jax.ShapeDtypeStruct((T, GDN_V), BF16),
            jax.ShapeDtypeStruct((T, LANES), F32),
            jax.ShapeDtypeStruct((T, 3 * DIFF_QK), BF16),
            jax.ShapeDtypeStruct((T, 2 * D_MODEL), BF16),
        ],
        scratch_shapes=[pltpu.VMEM((tm, D_MODEL), BF16), pltpu.VMEM((_HALO, 3 * GDN_QK), F32)],
        compiler_params=pltpu.CompilerParams(
            dimension_semantics=("arbitrary",), vmem_limit_bytes=VMEM_LIMIT),
        name="inproj",
    )(x2, g, w_gdn, w_ab, w_diff, conv_w, gparams)


def _split3_dot(x, sel):
    hi = x.astype(BF16)
    r1 = x - hi.astype(F32)
    mid = r1.astype(BF16)
    lo = (r1 - mid.astype(F32)).astype(BF16)
    return _dot(hi, sel) + _dot(mid, sel) + _dot(lo, sel)


def _pair(a, b):
    return jnp.concatenate([a, b], axis=1)


def _blockdiag2(x):
    c = x.shape[0]
    z = jnp.zeros((c, c), x.dtype)
    return jnp.concatenate([_pair(x[:, :c], z), _pair(z, x[:, c:])], axis=0)


def _gdn_kernel(q_ref, k_ref, v_ref, z_ref, g_ref, beta_ref, onorm_ref, o_ref,
                m2_s, n2_s, na_s, qe_s, au_s, gl_s, st_s):
    S = q_ref.shape[1]
    C = GDN_CHUNK
    NC = S // C

    ii = lax.broadcasted_iota(jnp.int32, (C, C), 0)
    jj = lax.broadcasted_iota(jnp.int32, (C, C), 1)
    ii2 = _pair(ii, ii)
    jj2 = _pair(jj, jj)
    strict2 = ii2 > jj2
    incl2 = ii2 >= jj2
    eye2 = (ii2 == jj2).astype(F32)

    gc = _split3_dot(g_ref[0, 0], (ii <= jj).astype(BF16))
    beta = beta_ref[0, 0]

    tiles = []
    for c in range(NC):
        gc_row = gc[c:c + 1, :]
        rg = jnp.broadcast_to(gc_row, (C, C))
        cg = rg.T
        glast = cg[C - 1:C, :]
        kf = k_ref[0, c * C:(c + 1) * C, :].astype(F32)
        eg = jnp.exp(cg)
        tiles.append(dict(
            decay=jnp.exp(jnp.minimum(cg - rg, 0.0)),
            beta_row=beta[c:c + 1, :],
            k=k_ref[0, c * C:(c + 1) * C, :],
            q=q_ref[0, c * C:(c + 1) * C, :],
            k_eg=(kf * eg).astype(BF16),
            q_dec=q_ref[0, c * C:(c + 1) * C, :].astype(F32) * eg,
            kd_t=(kf.T * (jnp.exp(glast - gc_row) * beta[c:c + 1, :])).astype(BF16),
            gl=jnp.exp(glast),
        ))

    pairs = [(tiles[c0], tiles[c0 + 1]) for c0 in range(0, NC, 2)]
    b_mats, aqs = [], []
    for ta, tb in pairs:
        k2 = _pair(ta["k"], tb["k"])
        kq = _dot_nt(jnp.concatenate([k2, _pair(ta["q"], tb["q"])], axis=0), _blockdiag2(k2))
        decay2 = _pair(ta["decay"], tb["decay"])
        beta2 = jnp.broadcast_to(_pair(ta["beta_row"], tb["beta_row"]), (C, 2 * C))
        b_mats.append(jnp.where(strict2, kq[0:C, :] * decay2, 0.0) * beta2)
        aqs.append((jnp.where(incl2, kq[C:2 * C, :] * decay2, 0.0) * beta2).astype(BF16))

    ps = [eye2 - jnp.where((ii2 >> 1) == (jj2 >> 1), b, 0.0) for b in b_mats]
    lvl = 1
    while (1 << lvl) < C:
        in_parent = (ii2 >> (lvl + 1)) == (jj2 >> (lvl + 1))
        in_child = (ii2 >> lvl) == (jj2 >> lvl)
        e_mask = jnp.logical_and(in_parent, jnp.logical_not(in_child))
        pbs = [p.astype(BF16) for p in ps]
        pes = [_dot(pb, _blockdiag2(jnp.where(e_mask, b, 0.0).astype(BF16)))
               for pb, b in zip(pbs, b_mats)]
        ps = [p - _dot(pe.astype(BF16), _blockdiag2(pb)) for p, pe, pb in zip(ps, pes, pbs)]
        lvl += 1

    wus = []
    for pi, (p, (ta, tb)) in enumerate(zip(ps, pairs)):
        pb = p.astype(BF16)
        for half, t in enumerate((ta, tb)):
            c = 2 * pi + half
            vc = v_ref[0, c * C:(c + 1) * C, :]
            wus.append(_dot(pb[:, half * C:(half + 1) * C], _pair(t["k_eg"], vc)).astype(BF16))

    mns = []
    for c, (t, wu) in enumerate(zip(tiles, wus)):
        aq_c = aqs[c // 2][:, (c % 2) * C:(c % 2 + 1) * C]
        r2 = _dot(jnp.concatenate([t["kd_t"], aq_c], axis=0), wu)
        mns.append(r2[0:C, :])
        qe_s[c] = (t["q_dec"] - r2[C:2 * C, 0:GDN_DK]).astype(BF16)
        au_s[c] = r2[C:2 * C, GDN_DK:GDN_DK + GDN_DV]

    for pi, (ta, tb) in enumerate(pairs):
        mn_a, mn_b = mns[2 * pi], mns[2 * pi + 1]
        ga, gb = ta["gl"], tb["gl"]
        mb = mn_b[:, 0:GDN_DK].astype(BF16)
        x = _dot(mb, mn_a.astype(BF16))
        m2 = gb * mn_a[:, 0:GDN_DK] + ga * mn_b[:, 0:GDN_DK] - x[:, 0:GDN_DK]
        m2_s[pi, 0:C, :] = m2.astype(BF16)
        m2_s[pi, C:2 * C, :] = mn_a[:, 0:GDN_DK].astype(BF16)
        n2_s[pi] = gb * mn_a[:, GDN_DK:] - x[:, GDN_DK:] + mn_b[:, GDN_DK:]
        na_s[pi] = mn_a[:, GDN_DK:]
        gl_s[pi, 0:8, :] = jnp.broadcast_to(ga * gb, (8, C))
        gl_s[pi, 8:16, :] = jnp.broadcast_to(ga, (8, C))

    def pair_step(pi, state):
        sb = state.astype(BF16)
        r = _dot(m2_s[pi], sb)
        gl = gl_s[pi]
        mid = state * gl[8:9, :] - r[C:2 * C, :] + na_s[pi]
        st_s[2 * pi] = sb
        st_s[2 * pi + 1] = mid.astype(BF16)
        return state * gl[0:1, :] - r[0:C, :] + n2_s[pi]

    lax.fori_loop(0, NC // 2, pair_step, jnp.zeros((GDN_DK, GDN_DV), F32))

    o = (_bdot(qe_s[...], st_s[...]) + au_s[...]).reshape(S, GDN_DV)
    ms = jnp.mean(o * o, axis=-1, keepdims=True)
    o = o * lax.rsqrt(ms + NORM_EPS) * onorm_ref[...]
    o_ref[0] = (o * z_ref[0].astype(F32)).astype(o_ref.dtype)


def _gdn(qkv3, zs3, gbt, onorm):
    B, S, _ = qkv3.shape
    NC = S // GDN_CHUNK
    H = GDN_HEADS
    C = GDN_CHUNK
    seq_spec = lambda off: pl.BlockSpec((1, S, LANES), lambda b, h, off=off: (b, 0, off + h))
    gate_spec = lambda off: pl.BlockSpec((1, 1, NC, C), lambda b, h, off=off: (b, off + h, 0, 0))
    return pl.pallas_call(
        _gdn_kernel,
        grid=(B, H),
        in_specs=[
            seq_spec(0), seq_spec(H), seq_spec(2 * H),
            pl.BlockSpec((1, S, LANES), lambda b, h: (b, 0, h)),
            gate_spec(0), gate_spec(H),
            pl.BlockSpec((1, LANES), lambda b, h: (0, 0)),
        ],
        out_specs=pl.BlockSpec((1, S, LANES), lambda b, h: (b, 0, h)),
        out_shape=jax.ShapeDtypeStruct((B, S, GDN_V), BF16),
        scratch_shapes=[
            pltpu.VMEM((NC // 2, 2 * GDN_DK, GDN_DK), BF16),
            pltpu.VMEM((NC // 2, GDN_DK, GDN_DV), F32),
            pltpu.VMEM((NC // 2, GDN_DK, GDN_DV), F32),
            pltpu.VMEM((NC, C, GDN_DK), BF16),
            pltpu.VMEM((NC, C, GDN_DV), F32),
            pltpu.VMEM((NC // 2, 16, LANES), F32),
            pltpu.VMEM((NC, GDN_DK, GDN_DV), BF16),
        ],
        compiler_params=pltpu.CompilerParams(
            dimension_semantics=("arbitrary", "arbitrary"), vmem_limit_bytes=VMEM_LIMIT),
        name="gdn",
    )(qkv3, qkv3, qkv3, zs3, gbt, gbt, onorm)


_ATT_BLOCK = 256
_LOG2E = math.log2(math.e)


def _diffattn_kernel(lam_ref, subln_ref, q_ref, k_ref, v_ref, o_ref, s_scr, p_scr, vext_scr,
                     *, lambda_init):
    S = q_ref.shape[1]
    tq = min(_ATT_BLOCK, S)
    dv = 2 * DIFF_DH
    neg = jnp.finfo(F32).min

    lp = lam_ref[...]
    lam = (jnp.exp(jnp.sum(lp[0:1, :] * lp[1:2, :], axis=-1, keepdims=True))
           - jnp.exp(jnp.sum(lp[2:3, :] * lp[3:4, :], axis=-1, keepdims=True))
           + lambda_init)

    vext_scr[:, 0:dv] = v_ref[0]
    vext_scr[:, dv:2 * dv] = jnp.ones((S, dv), BF16)

    lane = lax.broadcasted_iota(jnp.int32, (tq, dv), 1)
    causal = (lax.broadcasted_iota(jnp.int32, (2 * tq, tq), 1)
              <= jnp.bitwise_and(lax.broadcasted_iota(jnp.int32, (2 * tq, tq), 0), tq - 1))

    for qb in reversed(range(S // tq)):
        r0 = qb * tq
        n = r0 + tq
        s_buf = s_scr.at[qb % 2]
        p_buf = p_scr.at[qb % 2]
        qs = (q_ref[0, r0:n, :].astype(F32) * (DIFF_DH ** -0.5 * _LOG2E)).astype(BF16)
        zero = jnp.zeros_like(qs)
        qq = jnp.concatenate([jnp.where(lane < DIFF_DH, qs, zero),
                              jnp.where(lane >= DIFF_DH, qs, zero)], axis=0)

        m_run = None
        for c0 in range(0, n, tq):
            s = _dot_nt(qq, k_ref[0, c0:c0 + tq, :])
            if c0 == r0:
                s = jnp.where(causal, s, neg)
            s_buf[c0 // tq] = s
            for t in range(0, tq, LANES):
                tile = s[:, t:t + LANES]
                m_run = tile if m_run is None else jnp.maximum(m_run, tile)
        m_b = jnp.broadcast_to(jnp.max(m_run, axis=-1, keepdims=True), (2 * tq, LANES))

        for c0 in range(0, n, tq):
            for t in range(0, tq, LANES):
                p_buf[:, c0 + t:c0 + t + LANES] = jnp.exp2(
                    s_buf[c0 // tq, :, t:t + LANES] - m_b).astype(BF16)

        oe = _dot(p_buf[:, 0:n], vext_scr[0:n, :])
        a = (oe[0:tq, 0:dv] / oe[0:tq, dv:2 * dv]
             - lam * (oe[tq:2 * tq, 0:dv] / oe[tq:2 * tq, dv:2 * dv]))
        ms = jnp.mean(a * a, axis=-1, keepdims=True)
        a = a * lax.rsqrt(ms + NORM_EPS) * subln_ref[...] * (1.0 - lambda_init)
        o_ref[0, r0:n, :] = a.astype(o_ref.dtype)


def _diffattn(dqkv3, lam_params, subln, lambda_init):
    B, S, _ = dqkv3.shape
    H = DIFF_HEADS
    tq = min(_ATT_BLOCK, S)
    dv = 2 * DIFF_DH
    head = lambda off: pl.BlockSpec((1, S, dv), lambda b, h, off=off: (b, 0, off + h))
    return pl.pallas_call(
        functools.partial(_diffattn_kernel, lambda_init=lambda_init),
        grid=(B, H),
        in_specs=[
            pl.BlockSpec((8, DIFF_DH), lambda b, h: (0, 0)),
            pl.BlockSpec((1, dv), lambda b, h: (0, 0)),
            head(0), head(H), head(2 * H),
        ],
        out_specs=pl.BlockSpec((1, S, dv), lambda b, h: (b, 0, h)),
        out_shape=jax.ShapeDtypeStruct((B, S, DIFF_V), BF16),
        scratch_shapes=[
            pltpu.VMEM((2, S // tq, 2 * tq, tq), F32),
            pltpu.VMEM((2, 2 * tq, S), BF16),
            pltpu.VMEM((S, 2 * dv), BF16),
        ],
        compiler_params=pltpu.CompilerParams(
            dimension_semantics=("arbitrary", "arbitrary"), vmem_limit_bytes=VMEM_LIMIT),
        name="diffattn",
    )(lam_params, subln, dqkv3, dqkv3, dqkv3)


def _merge_kernel(x_ref, og_ref, od_ref, gates_ref, wbg_ref, wbd_ref, wout_ref,
                  npost_ref, npre_ref, x1_ref, h2_ref):
    mg = _dot(og_ref[...], wbg_ref[...])
    md = _dot(od_ref[...], wbd_ref[...])
    gg = _sigmoid(gates_ref[:, 0:D_MODEL].astype(F32))
    gd = _sigmoid(gates_ref[:, D_MODEL:2 * D_MODEL].astype(F32))
    merged = (gg * mg + gd * md).astype(BF16)
    m = _dot(merged, wout_ref[...])
    ms = jnp.mean(m * m, axis=-1, keepdims=True)
    x1 = x_ref[...] + m * lax.rsqrt(ms + NORM_EPS) * npost_ref[...]
    x1_ref[...] = x1
    ms1 = jnp.mean(x1 * x1, axis=-1, keepdims=True)
    h2_ref[...] = (x1 * lax.rsqrt(ms1 + NORM_EPS) * npre_ref[...]).astype(BF16)


def _merge(x2, og, od, gates, wbg, wbd, wout, npost, npre, tm):
    T = x2.shape[0]
    row = lambda w: pl.BlockSpec((tm, w), lambda i: (i, 0))
    return pl.pallas_call(
        _merge_kernel,
        grid=(T // tm,),
        in_specs=[
            row(D_MODEL), row(GDN_V), row(DIFF_V), row(2 * D_MODEL),
            _const_spec((GDN_V, D_MODEL)), _const_spec((DIFF_V, D_MODEL)),
            _const_spec((D_MODEL, D_MODEL)),
            _const_spec((1, D_MODEL)), _const_spec((1, D_MODEL)),
        ],
        out_specs=[row(D_MODEL), row(D_MODEL)],
        out_shape=[jax.ShapeDtypeStruct((T, D_MODEL), F32),
                   jax.ShapeDtypeStruct((T, D_MODEL), BF16)],
        compiler_params=pltpu.CompilerParams(
            dimension_semantics=("arbitrary",), vmem_limit_bytes=VMEM_LIMIT),
        name="merge",
    )(x2, og, od, gates, wbg, wbd, wout, npost, npre)


_FFN_CHUNK = 512


def _ffn_kernel(x1_ref, h2_ref, wup_ref, cw_ref, cb_ref, wdown_ref, npost_ref, out_ref,
                carry, act_scr):
    si = pl.program_id(1)
    tm = h2_ref.shape[1]

    @pl.when(si == 0)
    def _():
        carry[...] = jnp.zeros_like(carry)

    h2 = h2_ref[0]

    def conv_half(col0, cw):
        u = _dot(h2, wup_ref[:, col0:col0 + cw])
        prev = carry[:, col0:col0 + cw]
        carry[:, col0:col0 + cw] = u[tm - _HALO:tm, :]
        return _causal_conv(u, prev, cw_ref[:, col0:col0 + cw], FFN_CONV) + cb_ref[:, col0:col0 + cw]

    for c0 in range(0, D_FF, _FFN_CHUNK):
        cw = min(_FFN_CHUNK, D_FF - c0)
        gate = conv_half(c0, cw)
        up = conv_half(D_FF + c0, cw)
        act_scr[:, c0:c0 + cw] = (_silu(gate) * up).astype(BF16)

    f = _dot(act_scr[...], wdown_ref[...])
    ms = jnp.mean(f * f, axis=-1, keepdims=True)
    out_ref[0] = x1_ref[0] + f * lax.rsqrt(ms + NORM_EPS) * npost_ref[...]


def _ffn(x1_3, h2_3, wup, cw, cb, wdown, npost, tm):
    B, S, _ = x1_3.shape
    seq = lambda: pl.BlockSpec((1, tm, D_MODEL), lambda b, s: (b, s, 0))
    return pl.pallas_call(
        _ffn_kernel,
        grid=(B, S // tm),
        in_specs=[
            seq(), seq(),
            _const_spec((D_MODEL, 2 * D_FF)),
            _const_spec((FFN_CONV, 2 * D_FF)),
            _const_spec((1, 2 * D_FF)),
            _const_spec((D_FF, D_MODEL)),
            _const_spec((1, D_MODEL)),
        ],
        out_specs=seq(),
        out_shape=jax.ShapeDtypeStruct((B, S, D_MODEL), F32),
        scratch_shapes=[
            pltpu.VMEM((_HALO, 2 * D_FF), F32),
            pltpu.VMEM((tm, D_FF), BF16),
        ],
        compiler_params=pltpu.CompilerParams(
            dimension_semantics=("arbitrary", "arbitrary"), vmem_limit_bytes=VMEM_LIMIT),
        name="convffn",
    )(x1_3, h2_3, wup, cw, cb, wdown, npost)


def _row_tile(n, want):
    t = min(want, n)
    while n % t:
        t //= 2
    return t


def _layer(x, lambda_init, norm_mix_pre, w_in, conv_qkv_w, gdn_A_log, gdn_dt_bias, gdn_out_norm,
           lambda_q1, lambda_k1, lambda_q2, lambda_k2, diff_subln, w_branch_gdn, w_branch_diff,
           w_out, norm_mix_post, norm_ffn_pre, w_up, ffn_conv_w, ffn_conv_b, w_down, norm_ffn_post):
    B, S, D = x.shape
    T = B * S
    x2 = x.reshape(T, D)
    row2 = lambda v: v.reshape(1, -1).astype(F32)

    n_gdn = 4 * GDN_QK
    w_gdn = w_in[:, :n_gdn].astype(BF16)
    w_ab = jnp.pad(w_in[:, n_gdn:n_gdn + 2 * GDN_HEADS].astype(BF16),
                   ((0, 0), (0, LANES - 2 * GDN_HEADS)))
    w_diff = w_in[:, n_gdn + 2 * GDN_HEADS:].astype(BF16)
    gparams = jnp.zeros((8, LANES), F32)
    gparams = gparams.at[0, :GDN_HEADS].set(gdn_A_log.astype(F32))
    gparams = gparams.at[1, :GDN_HEADS].set(gdn_dt_bias.astype(F32))

    tm = _row_tile(S, 512)
    qkv, zs, gb, dqkv, gates = _inproj(x2, row2(norm_mix_pre), w_gdn, w_ab, w_diff,
                                       conv_qkv_w.astype(F32), gparams, tm, S // tm)

    gbt = jnp.transpose(gb.reshape(B, S, LANES)[:, :, :2 * GDN_HEADS], (0, 2, 1))
    gbt = gbt.reshape(B, 2 * GDN_HEADS, S // GDN_CHUNK, GDN_CHUNK)
    o_gdn = _gdn(qkv.reshape(B, S, -1), zs.reshape(B, S, -1), gbt, row2(gdn_out_norm))

    lam_params = jnp.zeros((8, DIFF_DH), F32)
    lam_params = lam_params.at[0].set(lambda_q1.astype(F32)).at[1].set(lambda_k1.astype(F32))
    lam_params = lam_params.at[2].set(lambda_q2.astype(F32)).at[3].set(lambda_k2.astype(F32))
    o_diff = _diffattn(dqkv.reshape(B, S, -1), lam_params, row2(diff_subln), lambda_init)

    x1, h2 = _merge(x2, o_gdn.reshape(T, -1), o_diff.reshape(T, -1), gates,
                    w_branch_gdn.astype(BF16), w_branch_diff.astype(BF16), w_out.astype(BF16),
                    row2(norm_mix_post), row2(norm_ffn_pre), _row_tile(T, 512))

    out = _ffn(x1.reshape(B, S, D), h2.reshape(B, S, D), w_up.astype(BF16),
               ffn_conv_w.astype(F32), row2(ffn_conv_b), w_down.astype(BF16),
               row2(norm_ffn_post), _row_tile(S, 512))
    return out


def kernel(x, norm_mix_pre, w_in, conv_qkv_w, gdn_A_log, gdn_dt_bias, gdn_out_norm, lambda_q1, lambda_k1, lambda_q2, lambda_k2, diff_subln, w_branch_gdn, w_branch_diff, w_out, norm_mix_post, norm_ffn_pre, w_up, ffn_conv_w, ffn_conv_b, w_down, norm_ffn_post):
    depth = w_in.shape[0]
    for l in range(depth):
        lambda_init = 0.8 - 0.6 * math.exp(-0.3 * l)
        x = _layer(x, lambda_init, norm_mix_pre[l], w_in[l], conv_qkv_w[l], gdn_A_log[l],
                   gdn_dt_bias[l], gdn_out_norm[l], lambda_q1[l], lambda_k1[l], lambda_q2[l],
                   lambda_k2[l], diff_subln[l], w_branch_gdn[l], w_branch_diff[l], w_out[l],
                   norm_mix_post[l], norm_ffn_pre[l], w_up[l], ffn_conv_w[l], ffn_conv_b[l],
                   w_down[l], norm_ffn_post[l])
    return x
```

```python
import functools
import math

import jax
import jax.numpy as jnp
from jax import lax
from jax.experimental import pallas as pl
from jax.experimental.pallas import tpu as pltpu

F32 = jnp.float32
BF16 = jnp.bfloat16

D_MODEL = 1024
GDN_HEADS = 8
GDN_DK = 128
GDN_DV = 128
GDN_CONV = 4
DIFF_HEADS = 8
DIFF_DH = 64
D_FF = 2816
FFN_CONV = 3
NORM_EPS = 1e-6

GDN_QK = GDN_HEADS * GDN_DK
GDN_V = GDN_HEADS * GDN_DV
DIFF_QK = DIFF_HEADS * 2 * DIFF_DH
DIFF_V = DIFF_HEADS * 2 * DIFF_DH

LANES = 128
GDN_CHUNK = 128
VMEM_LIMIT = 56 * 1024 * 1024


def _dot(a, b):
    return jnp.dot(a, b, preferred_element_type=F32)


def _dot_nt(a, b):
    return lax.dot_general(a, b, (((1,), (1,)), ((), ())), preferred_element_type=F32)


def _bdot(a, b):
    return lax.dot_general(a, b, (((2,), (1,)), ((0,), (0,))), preferred_element_type=F32)


def _sigmoid(x):
    return 1.0 / (1.0 + jnp.exp(-x))


def _silu(x):
    hx = 0.5 * x
    return hx * jnp.tanh(hx) + hx


def _softplus(x):
    return jnp.maximum(x, 0.0) + jnp.log1p(jnp.exp(-jnp.abs(x)))


def _const_spec(shape):
    nd = len(shape)
    return pl.BlockSpec(shape, lambda *_: (0,) * nd, pipeline_mode=pl.Buffered(1))


_HALO = 8


def _causal_conv(u, prev, w, taps):
    tm, cw = u.shape
    row = lax.broadcasted_iota(jnp.int32, prev.shape, 0)
    u3 = u.reshape(tm // _HALO, _HALO, cw)
    y = u3 * w[taps - 1:taps, :]
    for sh in range(1, taps):
        r = pltpu.roll(u3, sh, 1)
        before = jnp.concatenate([pltpu.roll(prev, sh, 0)[None], r[:-1]], axis=0)
        y = y + jnp.where(row < sh, before, r) * w[taps - 1 - sh:taps - sh, :]
    return y.reshape(tm, cw)


_IN_CHUNK = 512


def _inproj_kernel(x_ref, g_ref, wg_ref, wab_ref, wd_ref, cw_ref, gp_ref,
                   qkv_ref, z_ref, gb_ref, dqkv_ref, gates_ref, h_scr, carry, *, tiles_per_seq):
    tm = x_ref.shape[0]

    @pl.when(pl.program_id(0) % tiles_per_seq == 0)
    def _():
        carry[...] = jnp.zeros_like(carry)

    x = x_ref[...]
    ms = jnp.mean(x * x, axis=-1, keepdims=True)
    h_scr[...] = (x * lax.rsqrt(ms + NORM_EPS) * g_ref[...]).astype(BF16)

    plain = ([(dqkv_ref, c0, c0) for c0 in range(0, 3 * DIFF_QK, _IN_CHUNK)]
             + [(gates_ref, c0, 3 * DIFF_QK + c0) for c0 in range(0, 2 * D_MODEL, _IN_CHUNK)])

    def emit_plain(count):
        for _ in range(min(count, len(plain))):
            out_ref, c0, w_off = plain.pop(0)
            acc = _dot(h_scr[...], wd_ref[:, w_off:w_off + _IN_CHUNK])
            out_ref[:, c0:c0 + _IN_CHUNK] = acc.astype(out_ref.dtype)

    for c0 in range(0, 3 * GDN_QK, _IN_CHUNK):
        emit_plain(1)
        u = _dot(h_scr[...], wg_ref[:, c0:c0 + _IN_CHUNK])
        prev = carry[:, c0:c0 + _IN_CHUNK]
        carry[:, c0:c0 + _IN_CHUNK] = u[tm - _HALO:tm, :]
        y = _silu(_causal_conv(u, prev, cw_ref[:, c0:c0 + _IN_CHUNK], GDN_CONV))
        if c0 < 2 * GDN_QK:
            scale = GDN_DK ** -0.5 if c0 < GDN_QK else 1.0
            heads = []
            for h0 in range(0, _IN_CHUNK, GDN_DK):
                yh = y[:, h0:h0 + GDN_DK]
                inv = lax.rsqrt(jnp.sum(yh * yh, axis=-1, keepdims=True) + NORM_EPS)
                heads.append(yh * (inv * scale))
            y = jnp.concatenate(heads, axis=1)
        qkv_ref[:, c0:c0 + _IN_CHUNK] = y.astype(BF16)

    for c0 in range(0, GDN_V, _IN_CHUNK):
        emit_plain(1)
        zc = _dot(h_scr[...], wg_ref[:, 3 * GDN_QK + c0:3 * GDN_QK + c0 + _IN_CHUNK])
        z_ref[:, c0:c0 + _IN_CHUNK] = _silu(zc).astype(BF16)

    ab = _dot(h_scr[...], wab_ref[...])
    gp = gp_ref[...]
    g = -jnp.exp(gp[0:1, :]) * _softplus(ab + gp[1:2, :])
    lane = lax.broadcasted_iota(jnp.int32, ab.shape, 1)
    gb_ref[...] = jnp.where(lane < GDN_HEADS, g, _sigmoid(ab))
    emit_plain(len(plain))


def _inproj(x2, g, w_gdn, w_ab, w_diff, conv_w, gparams, tm, tiles_per_seq):
    T = x2.shape[0]
    rows = lambda w: pl.BlockSpec((tm, w), lambda i: (i, 0))
    return pl.pallas_call(
        functools.partial(_inproj_kernel, tiles_per_seq=tiles_per_seq),
        grid=(T // tm,),
        in_specs=[
            rows(D_MODEL),
            _const_spec((1, D_MODEL)),
            _const_spec(w_gdn.shape),
            _const_spec(w_ab.shape),
            _const_spec(w_diff.shape),
            _const_spec((GDN_CONV, 3 * GDN_QK)),
            _const_spec((8, LANES)),
        ],
        out_specs=[rows(3 * GDN_QK), rows(GDN_V), rows(LANES), rows(3 * DIFF_QK), rows(2 * D_MODEL)],
        out_shape=[
            jax.ShapeDtypeStruct((T, 3 * GDN_QK), BF16),
            jax.ShapeDtypeStruct((T, GDN_V), BF16),
            jax.ShapeDtypeStruct((T, LANES), F32),
            jax.ShapeDtypeStruct((T, 3 * DIFF_QK), BF16),
            jax.ShapeDtypeStruct((T, 2 * D_MODEL), BF16),
        ],
        scratch_shapes=[pltpu.VMEM((tm, D_MODEL), BF16), pltpu.VMEM((_HALO, 3 * GDN_QK), F32)],
        compiler_params=pltpu.CompilerParams(
            dimension_semantics=("arbitrary",), vmem_limit_bytes=VMEM_LIMIT),
        name="inproj",
    )(x2, g, w_gdn, w_ab, w_diff, conv_w, gparams)


def _split3_dot(x, sel):
    hi = x.astype(BF16)
    r1 = x - hi.astype(F32)
    mid = r1.astype(BF16)
    lo = (r1 - mid.astype(F32)).astype(BF16)
    return _dot(hi, sel) + _dot(mid, sel) + _dot(lo, sel)


def _pair(a, b):
    return jnp.concatenate([a, b], axis=1)


def _blockdiag2(x):
    c = x.shape[0]
    z = jnp.zeros((c, c), x.dtype)
    return jnp.concatenate([_pair(x[:, :c], z), _pair(z, x[:, c:])], axis=0)


def _gdn_kernel(q_ref, k_ref, v_ref, z_ref, g_ref, beta_ref, onorm_ref, o_ref):
    S = q_ref.shape[1]
    C = GDN_CHUNK
    NC = S // C

    ii = lax.broadcasted_iota(jnp.int32, (C, C), 0)
    jj = lax.broadcasted_iota(jnp.int32, (C, C), 1)
    ii2 = _pair(ii, ii)
    jj2 = _pair(jj, jj)
    strict2 = ii2 > jj2
    incl2 = ii2 >= jj2
    eye2 = (ii2 == jj2).astype(F32)

    gc = _split3_dot(g_ref[0, 0], (ii <= jj).astype(BF16))
    beta = beta_ref[0, 0]

    tiles = []
    for c in range(NC):
        gc_row = gc[c:c + 1, :]
        rg = jnp.broadcast_to(gc_row, (C, C))
        cg = rg.T
        glast = cg[C - 1:C, :]
        kf = k_ref[0, c * C:(c + 1) * C, :].astype(F32)
        eg = jnp.exp(cg)
        tiles.append(dict(
            decay=jnp.exp(jnp.minimum(cg - rg, 0.0)),
            beta_row=beta[c:c + 1, :],
            k=k_ref[0, c * C:(c + 1) * C, :],
            q=q_ref[0, c * C:(c + 1) * C, :],
            k_eg=(kf * eg).astype(BF16),
            q_dec=q_ref[0, c * C:(c + 1) * C, :].astype(F32) * eg,
            kd_t=(kf.T * (jnp.exp(glast - gc_row) * beta[c:c + 1, :])).astype(BF16),
            gl=jnp.exp(glast),
        ))

    pairs = [(tiles[c0], tiles[c0 + 1]) for c0 in range(0, NC, 2)]
    b_mats, aqs = [], []
    for ta, tb in pairs:
        k2 = _pair(ta["k"], tb["k"])
        kq = _dot_nt(jnp.concatenate([k2, _pair(ta["q"], tb["q"])], axis=0), _blockdiag2(k2))
        decay2 = _pair(ta["decay"], tb["decay"])
        beta2 = jnp.broadcast_to(_pair(ta["beta_row"], tb["beta_row"]), (C, 2 * C))
        b_mats.append(jnp.where(strict2, kq[0:C, :] * decay2, 0.0) * beta2)
        aqs.append((jnp.where(incl2, kq[C:2 * C, :] * decay2, 0.0) * beta2).astype(BF16))

    ps = [eye2 - jnp.where((ii2 >> 1) == (jj2 >> 1), b, 0.0) for b in b_mats]
    lvl = 1
    while (1 << lvl) < C:
        in_parent = (ii2 >> (lvl + 1)) == (jj2 >> (lvl + 1))
        in_child = (ii2 >> lvl) == (jj2 >> lvl)
        e_mask = jnp.logical_and(in_parent, jnp.logical_not(in_child))
        pbs = [p.astype(BF16) for p in ps]
        pes = [_dot(pb, _blockdiag2(jnp.where(e_mask, b, 0.0).astype(BF16)))
               for pb, b in zip(pbs, b_mats)]
        ps = [p - _dot(pe.astype(BF16), _blockdiag2(pb)) for p, pe, pb in zip(ps, pes, pbs)]
        lvl += 1

    wus = []
    for pi, (p, (ta, tb)) in enumerate(zip(ps, pairs)):
        pb = p.astype(BF16)
        for half, t in enumerate((ta, tb)):
            c = 2 * pi + half
            vc = v_ref[0, c * C:(c + 1) * C, :]
            wus.append(_dot(pb[:, half * C:(half + 1) * C], _pair(t["k_eg"], vc)).astype(BF16))

    mns, qes, aus = [], [], []
    for c, (t, wu) in enumerate(zip(tiles, wus)):
        aq_c = aqs[c // 2][:, (c % 2) * C:(c % 2 + 1) * C]
        r2 = _dot(jnp.concatenate([t["kd_t"], aq_c], axis=0), wu)
        mns.append(r2[0:C, :])
        qes.append((t["q_dec"] - r2[C:2 * C, 0:GDN_DK]).astype(BF16))
        aus.append(r2[C:2 * C, GDN_DK:GDN_DK + GDN_DV])

    steps = []
    for pi, (ta, tb) in enumerate(pairs):
        mn_a, mn_b = mns[2 * pi], mns[2 * pi + 1]
        ga, gb = ta["gl"], tb["gl"]
        mb = mn_b[:, 0:GDN_DK].astype(BF16)
        x = _dot(mb, mn_a.astype(BF16))
        m2 = gb * mn_a[:, 0:GDN_DK] + ga * mn_b[:, 0:GDN_DK] - x[:, 0:GDN_DK]
        steps.append(dict(
            m=jnp.concatenate([m2.astype(BF16), mn_a[:, 0:GDN_DK].astype(BF16)], axis=0),
            n2=gb * mn_a[:, GDN_DK:] - x[:, GDN_DK:] + mn_b[:, GDN_DK:],
            na=mn_a[:, GDN_DK:], g2=ga * gb, ga=ga))

    def emit_out(c, sb):
        o = _dot(qes[c], sb) + aus[c]
        ms = jnp.mean(o * o, axis=-1, keepdims=True)
        o = o * lax.rsqrt(ms + NORM_EPS) * onorm_ref[...]
        rows = slice(c * C, (c + 1) * C)
        o_ref[0, rows, :] = (o * z_ref[0, rows, :].astype(F32)).astype(o_ref.dtype)

    state = jnp.zeros((GDN_DK, GDN_DV), F32)
    for pi, st in enumerate(steps):
        sb = state.astype(BF16)
        r = _dot(st["m"], sb)
        mid = state * st["ga"] - r[C:2 * C, :] + st["na"]
        emit_out(2 * pi, sb)
        emit_out(2 * pi + 1, mid.astype(BF16))
        state = state * st["g2"] - r[0:C, :] + st["n2"]


def _gdn(qkv3, zs3, gbt, onorm):
    B, S, _ = qkv3.shape
    NC = S // GDN_CHUNK
    H = GDN_HEADS
    C = GDN_CHUNK
    seq_spec = lambda off: pl.BlockSpec((1, S, LANES), lambda b, h, off=off: (b, 0, off + h))
    gate_spec = lambda off: pl.BlockSpec((1, 1, NC, C), lambda b, h, off=off: (b, off + h, 0, 0))
    return pl.pallas_call(
        _gdn_kernel,
        grid=(B, H),
        in_specs=[
            seq_spec(0), seq_spec(H), seq_spec(2 * H),
            pl.BlockSpec((1, S, LANES), lambda b, h: (b, 0, h)),
            gate_spec(0), gate_spec(H),
            pl.BlockSpec((1, LANES), lambda b, h: (0, 0)),
        ],
        out_specs=pl.BlockSpec((1, S, LANES), lambda b, h: (b, 0, h)),
        out_shape=jax.ShapeDtypeStruct((B, S, GDN_V), BF16),
        compiler_params=pltpu.CompilerParams(
            dimension_semantics=("arbitrary", "arbitrary"), vmem_limit_bytes=VMEM_LIMIT),
        name="gdn",
    )(qkv3, qkv3, qkv3, zs3, gbt, gbt, onorm)


_ATT_BLOCK = 256
_LOG2E = math.log2(math.e)


def _diffattn_kernel(lam_ref, subln_ref, q_ref, k_ref, v_ref, o_ref, s_scr, p_scr, vext_scr,
                     *, lambda_init):
    S = q_ref.shape[1]
    tq = min(_ATT_BLOCK, S)
    dv = 2 * DIFF_DH
    neg = jnp.finfo(F32).min

    lp = lam_ref[...]
    lam = (jnp.exp(jnp.sum(lp[0:1, :] * lp[1:2, :], axis=-1, keepdims=True))
           - jnp.exp(jnp.sum(lp[2:3, :] * lp[3:4, :], axis=-1, keepdims=True))
           + lambda_init)

    vext_scr[:, 0:dv] = v_ref[0]
    vext_scr[:, dv:2 * dv] = jnp.ones((S, dv), BF16)

    lane = lax.broadcasted_iota(jnp.int32, (tq, dv), 1)
    causal = (lax.broadcasted_iota(jnp.int32, (2 * tq, tq), 1)
              <= jnp.bitwise_and(lax.broadcasted_iota(jnp.int32, (2 * tq, tq), 0), tq - 1))

    for qb in reversed(range(S // tq)):
        r0 = qb * tq
        n = r0 + tq
        s_buf = s_scr.at[qb % 2]
        p_buf = p_scr.at[qb % 2]
        qs = (q_ref[0, r0:n, :].astype(F32) * (DIFF_DH ** -0.5 * _LOG2E)).astype(BF16)
        zero = jnp.zeros_like(qs)
        qq = jnp.concatenate([jnp.where(lane < DIFF_DH, qs, zero),
                              jnp.where(lane >= DIFF_DH, qs, zero)], axis=0)

        m_run = None
        for c0 in range(0, n, tq):
            s = _dot_nt(qq, k_ref[0, c0:c0 + tq, :])
            if c0 == r0:
                s = jnp.where(causal, s, neg)
            s_buf[c0 // tq] = s
            for t in range(0, tq, LANES):
                tile = s[:, t:t + LANES]
                m_run = tile if m_run is None else jnp.maximum(m_run, tile)
        m_b = jnp.broadcast_to(jnp.max(m_run, axis=-1, keepdims=True), (2 * tq, LANES))

        for c0 in range(0, n, tq):
            for t in range(0, tq, LANES):
                p_buf[:, c0 + t:c0 + t + LANES] = jnp.exp2(
                    s_buf[c0 // tq, :, t:t + LANES] - m_b).astype(BF16)

        oe = _dot(p_buf[:, 0:n], vext_scr[0:n, :])
        a = (oe[0:tq, 0:dv] / oe[0:tq, dv:2 * dv]
             - lam * (oe[tq:2 * tq, 0:dv] / oe[tq:2 * tq, dv:2 * dv]))
        ms = jnp.mean(a * a, axis=-1, keepdims=True)
        a = a * lax.rsqrt(ms + NORM_EPS) * subln_ref[...] * (1.0 - lambda_init)
        o_ref[0, r0:n, :] = a.astype(o_ref.dtype)


def _diffattn(dqkv3, lam_params, subln, lambda_init):
    B, S, _ = dqkv3.shape
    H = DIFF_HEADS
    tq = min(_ATT_BLOCK, S)
    dv = 2 * DIFF_DH
    head = lambda off: pl.BlockSpec((1, S, dv), lambda b, h, off=off: (b, 0, off + h))
    return pl.pallas_call(
        functools.partial(_diffattn_kernel, lambda_init=lambda_init),
        grid=(B, H),
        in_specs=[
            pl.BlockSpec((8, DIFF_DH), lambda b, h: (0, 0)),
            pl.BlockSpec((1, dv), lambda b, h: (0, 0)),
            head(0), head(H), head(2 * H),
        ],
        out_specs=pl.BlockSpec((1, S, dv), lambda b, h: (b, 0, h)),
        out_shape=jax.ShapeDtypeStruct((B, S, DIFF_V), BF16),
        scratch_shapes=[
            pltpu.VMEM((2, S // tq, 2 * tq, tq), F32),
            pltpu.VMEM((2, 2 * tq, S), BF16),
            pltpu.VMEM((S, 2 * dv), BF16),
        ],
        compiler_params=pltpu.CompilerParams(
            dimension_semantics=("arbitrary", "arbitrary"), vmem_limit_bytes=VMEM_LIMIT),
        name="diffattn",
    )(lam_params, subln, dqkv3, dqkv3, dqkv3)


def _merge_kernel(x_ref, og_ref, od_ref, gates_ref, wbg_ref, wbd_ref, wout_ref,
                  npost_ref, npre_ref, x1_ref, h2_ref):
    tm = x_ref.shape[0]
    half = tm // 2
    for r0 in (0, half):
        rs = slice(r0, r0 + half)
        mg = _dot(og_ref[rs, :], wbg_ref[...])
        md = _dot(od_ref[rs, :], wbd_ref[...])
        gg = _sigmoid(gates_ref[rs, 0:D_MODEL].astype(F32))
        gd = _sigmoid(gates_ref[rs, D_MODEL:2 * D_MODEL].astype(F32))
        merged = (gg * mg + gd * md).astype(BF16)
        m = _dot(merged, wout_ref[...])
        ms = jnp.mean(m * m, axis=-1, keepdims=True)
        x1 = x_ref[rs, :] + m * lax.rsqrt(ms + NORM_EPS) * npost_ref[...]
        x1_ref[rs, :] = x1
        ms1 = jnp.mean(x1 * x1, axis=-1, keepdims=True)
        h2_ref[rs, :] = (x1 * lax.rsqrt(ms1 + NORM_EPS) * npre_ref[...]).astype(BF16)


def _merge(x2, og, od, gates, wbg, wbd, wout, npost, npre, tm):
    T = x2.shape[0]
    row = lambda w: pl.BlockSpec((tm, w), lambda i: (i, 0))
    return pl.pallas_call(
        _merge_kernel,
        grid=(T // tm,),
        in_specs=[
            row(D_MODEL), row(GDN_V), row(DIFF_V), row(2 * D_MODEL),
            _const_spec((GDN_V, D_MODEL)), _const_spec((DIFF_V, D_MODEL)),
            _const_spec((D_MODEL, D_MODEL)),
            _const_spec((1, D_MODEL)), _const_spec((1, D_MODEL)),
        ],
        out_specs=[row(D_MODEL), row(D_MODEL)],
        out_shape=[jax.ShapeDtypeStruct((T, D_MODEL), F32),
                   jax.ShapeDtypeStruct((T, D_MODEL), BF16)],
        compiler_params=pltpu.CompilerParams(
            dimension_semantics=("arbitrary",), vmem_limit_bytes=VMEM_LIMIT),
        name="merge",
    )(x2, og, od, gates, wbg, wbd, wout, npost, npre)


_FFN_CHUNK = 512


def _ffn_kernel(x1_ref, h2_ref, wup_ref, cw_ref, cb_ref, wdown_ref, npost_ref, out_ref,
                carry, act_scr):
    si = pl.program_id(1)
    tm = h2_ref.shape[1]

    @pl.when(si == 0)
    def _():
        carry[...] = jnp.zeros_like(carry)

    h2 = h2_ref[0]

    def conv_half(col0, cw):
        u = _dot(h2, wup_ref[:, col0:col0 + cw])
        prev = carry[:, col0:col0 + cw]
        carry[:, col0:col0 + cw] = u[tm - _HALO:tm, :]
        return _causal_conv(u, prev, cw_ref[:, col0:col0 + cw], FFN_CONV) + cb_ref[:, col0:col0 + cw]

    for c0 in range(0, D_FF, _FFN_CHUNK):
        cw = min(_FFN_CHUNK, D_FF - c0)
        gate = conv_half(c0, cw)
        up = conv_half(D_FF + c0, cw)
        act_scr[:, c0:c0 + cw] = (_silu(gate) * up).astype(BF16)

    f = _dot(act_scr[...], wdown_ref[...])
    ms = jnp.mean(f * f, axis=-1, keepdims=True)
    out_ref[0] = x1_ref[0] + f * lax.rsqrt(ms + NORM_EPS) * npost_ref[...]


def _ffn(x1_3, h2_3, wup, cw, cb, wdown, npost, tm):
    B, S, _ = x1_3.shape
    seq = lambda: pl.BlockSpec((1, tm, D_MODEL), lambda b, s: (b, s, 0))
    return pl.pallas_call(
        _ffn_kernel,
        grid=(B, S // tm),
        in_specs=[
            seq(), seq(),
            _const_spec((D_MODEL, 2 * D_FF)),
            _const_spec((FFN_CONV, 2 * D_FF)),
            _const_spec((1, 2 * D_FF)),
            _const_spec((D_FF, D_MODEL)),
            _const_spec((1, D_MODEL)),
        ],
        out_specs=seq(),
        out_shape=jax.ShapeDtypeStruct((B, S, D_MODEL), F32),
        scratch_shapes=[
            pltpu.VMEM((_HALO, 2 * D_FF), F32),
            pltpu.VMEM((tm, D_FF), BF16),
        ],
        compiler_params=pltpu.CompilerParams(
            dimension_semantics=("arbitrary", "arbitrary"), vmem_limit_bytes=VMEM_LIMIT),
        name="convffn",
    )(x1_3, h2_3, wup, cw, cb, wdown, npost)


def _row_tile(n, want):
    t = min(want, n)
    while n % t:
        t //= 2
    return t


def _layer(x, lambda_init, norm_mix_pre, w_in, conv_qkv_w, gdn_A_log, gdn_dt_bias, gdn_out_norm,
           lambda_q1, lambda_k1, lambda_q2, lambda_k2, diff_subln, w_branch_gdn, w_branch_diff,
           w_out, norm_mix_post, norm_ffn_pre, w_up, ffn_conv_w, ffn_conv_b, w_down, norm_ffn_post):
    B, S, D = x.shape
    T = B * S
    x2 = x.reshape(T, D)
    row2 = lambda v: v.reshape(1, -1).astype(F32)

    n_gdn = 4 * GDN_QK
    w_gdn = w_in[:, :n_gdn].astype(BF16)
    w_ab = jnp.pad(w_in[:, n_gdn:n_gdn + 2 * GDN_HEADS].astype(BF16),
                   ((0, 0), (0, LANES - 2 * GDN_HEADS)))
    w_diff = w_in[:, n_gdn + 2 * GDN_HEADS:].astype(BF16)
    gparams = jnp.zeros((8, LANES), F32)
    gparams = gparams.at[0, :GDN_HEADS].set(gdn_A_log.astype(F32))
    gparams = gparams.at[1, :GDN_HEADS].set(gdn_dt_bias.astype(F32))

    tm = _row_tile(S, 512)
    qkv, zs, gb, dqkv, gates = _inproj(x2, row2(norm_mix_pre), w_gdn, w_ab, w_diff,
                                       conv_qkv_w.astype(F32), gparams, tm, S // tm)

    gbt = jnp.transpose(gb.reshape(B, S, LANES)[:, :, :2 * GDN_HEADS], (0, 2, 1))
    gbt = gbt.reshape(B, 2 * GDN_HEADS, S // GDN_CHUNK, GDN_CHUNK)
    o_gdn = _gdn(qkv.reshape(B, S, -1), zs.reshape(B, S, -1), gbt, row2(gdn_out_norm))

    lam_params = jnp.zeros((8, DIFF_DH), F32)
    lam_params = lam_params.at[0].set(lambda_q1.astype(F32)).at[1].set(lambda_k1.astype(F32))
    lam_params = lam_params.at[2].set(lambda_q2.astype(F32)).at[3].set(lambda_k2.astype(F32))
    o_diff = _diffattn(dqkv.reshape(B, S, -1), lam_params, row2(diff_subln), lambda_init)

    x1, h2 = _merge(x2, o_gdn.reshape(T, -1), o_diff.reshape(T, -1), gates,
                    w_branch_gdn.astype(BF16), w_branch_diff.astype(BF16), w_out.astype(BF16),
                    row2(norm_mix_post), row2(norm_ffn_pre), _row_tile(T, 1024))

    out = _ffn(x1.reshape(B, S, D), h2.reshape(B, S, D), w_up.astype(BF16),
               ffn_conv_w.astype(F32), row2(ffn_conv_b), w_down.astype(BF16),
               row2(norm_ffn_post), _row_tile(S, 512))
    return out


def kernel(x, norm_mix_pre, w_in, conv_qkv_w, gdn_A_log, gdn_dt_bias, gdn_out_norm, lambda_q1, lambda_k1, lambda_q2, lambda_k2, diff_subln, w_branch_gdn, w_branch_diff, w_out, norm_mix_post, norm_ffn_pre, w_up, ffn_conv_w, ffn_conv_b, w_down, norm_ffn_post):
    depth = w_in.shape[0]
    for l in range(depth):
        lambda_init = 0.8 - 0.6 * math.exp(-0.3 * l)
        x = _layer(x, lambda_init, norm_mix_pre[l], w_in[l], conv_qkv_w[l], gdn_A_log[l],
                   gdn_dt_bias[l], gdn_out_norm[l], lambda_q1[l], lambda_k1[l], lambda_q2[l],
                   lambda_k2[l], diff_subln[l], w_branch_gdn[l], w_branch_diff[l], w_out[l],
                   norm_mix_post[l], norm_ffn_pre[l], w_up[l], ffn_conv_w[l], ffn_conv_b[l],
                   w_down[l], norm_ffn_post[l])
    return x
```

```python
import functools
import math

import jax
import jax.numpy as jnp
from jax import lax
from jax.experimental import pallas as pl
from jax.experimental.pallas import tpu as pltpu

F32 = jnp.float32
BF16 = jnp.bfloat16

D_MODEL = 1024
GDN_HEADS = 8
GDN_DK = 128
GDN_DV = 128
GDN_CONV = 4
DIFF_HEADS = 8
DIFF_DH = 64
D_FF = 2816
FFN_CONV = 3
NORM_EPS = 1e-6

GDN_QK = GDN_HEADS * GDN_DK
GDN_V = GDN_HEADS * GDN_DV
DIFF_QK = DIFF_HEADS * 2 * DIFF_DH
DIFF_V = DIFF_HEADS * 2 * DIFF_DH

LANES = 128
GDN_CHUNK = 128
VMEM_LIMIT = 56 * 1024 * 1024


def _dot(a, b):
    return jnp.dot(a, b, preferred_element_type=F32)


def _dot_nt(a, b):
    return lax.dot_general(a, b, (((1,), (1,)), ((), ())), preferred_element_type=F32)


def _bdot(a, b):
    return lax.dot_general(a, b, (((2,), (1,)), ((0,), (0,))), preferred_element_type=F32)


def _sigmoid(x):
    return 1.0 / (1.0 + jnp.exp(-x))


def _silu(x):
    hx = 0.5 * x
    return hx * jnp.tanh(hx) + hx


def _softplus(x):
    return jnp.maximum(x, 0.0) + jnp.log1p(jnp.exp(-jnp.abs(x)))


def _const_spec(shape):
    nd = len(shape)
    return pl.BlockSpec(shape, lambda *_: (0,) * nd, pipeline_mode=pl.Buffered(1))


_HALO = 8


def _causal_conv(u, prev, w, taps):
    tm, cw = u.shape
    row = lax.broadcasted_iota(jnp.int32, prev.shape, 0)
    u3 = u.reshape(tm // _HALO, _HALO, cw)
    y = u3 * w[taps - 1:taps, :]
    for sh in range(1, taps):
        r = pltpu.roll(u3, sh, 1)
        before = jnp.concatenate([pltpu.roll(prev, sh, 0)[None], r[:-1]], axis=0)
        y = y + jnp.where(row < sh, before, r) * w[taps - 1 - sh:taps - sh, :]
    return y.reshape(tm, cw)


_IN_CHUNK = 512


def _inproj_kernel(x_ref, g_ref, wg_ref, wab_ref, wd_ref, cw_ref, gp_ref,
                   qkv_ref, z_ref, gb_ref, dqkv_ref, gates_ref, h_scr, carry, *, tiles_per_seq):
    tm = x_ref.shape[0]

    @pl.when(pl.program_id(0) % tiles_per_seq == 0)
    def _():
        carry[...] = jnp.zeros_like(carry)

    x = x_ref[...]
    ms = jnp.mean(x * x, axis=-1, keepdims=True)
    h_scr[...] = (x * lax.rsqrt(ms + NORM_EPS) * g_ref[...]).astype(BF16)

    plain = ([(dqkv_ref, c0, c0) for c0 in range(0, 3 * DIFF_QK, _IN_CHUNK)]
             + [(gates_ref, c0, 3 * DIFF_QK + c0) for c0 in range(0, 2 * D_MODEL, _IN_CHUNK)])

    def emit_plain(count):
        for _ in range(min(count, len(plain))):
            out_ref, c0, w_off = plain.pop(0)
            acc = _dot(h_scr[...], wd_ref[:, w_off:w_off + _IN_CHUNK])
            out_ref[:, c0:c0 + _IN_CHUNK] = acc.astype(out_ref.dtype)

    for c0 in range(0, 3 * GDN_QK, _IN_CHUNK):
        emit_plain(1)
        u = _dot(h_scr[...], wg_ref[:, c0:c0 + _IN_CHUNK])
        prev = carry[:, c0:c0 + _IN_CHUNK]
        carry[:, c0:c0 + _IN_CHUNK] = u[tm - _HALO:tm, :]
        y = _silu(_causal_conv(u, prev, cw_ref[:, c0:c0 + _IN_CHUNK], GDN_CONV))
        if c0 < 2 * GDN_QK:
            scale = GDN_DK ** -0.5 if c0 < GDN_QK else 1.0
            heads = []
            for h0 in range(0, _IN_CHUNK, GDN_DK):
                yh = y[:, h0:h0 + GDN_DK]
                inv = lax.rsqrt(jnp.sum(yh * yh, axis=-1, keepdims=True) + NORM_EPS)
                heads.append(yh * (inv * scale))
            y = jnp.concatenate(heads, axis=1)
        qkv_ref[:, c0:c0 + _IN_CHUNK] = y.astype(BF16)

    for c0 in range(0, GDN_V, _IN_CHUNK):
        emit_plain(1)
        zc = _dot(h_scr[...], wg_ref[:, 3 * GDN_QK + c0:3 * GDN_QK + c0 + _IN_CHUNK])
        z_ref[:, c0:c0 + _IN_CHUNK] = _silu(zc).astype(BF16)

    ab = _dot(h_scr[...], wab_ref[...])
    gp = gp_ref[...]
    g = -jnp.exp(gp[0:1, :]) * _softplus(ab + gp[1:2, :])
    lane = lax.broadcasted_iota(jnp.int32, ab.shape, 1)
    gb_ref[...] = jnp.where(lane < GDN_HEADS, g, _sigmoid(ab))
    emit_plain(len(plain))


def _inproj(x2, g, w_gdn, w_ab, w_diff, conv_w, gparams, tm, tiles_per_seq):
    T = x2.shape[0]
    rows = lambda w: pl.BlockSpec((tm, w), lambda i: (i, 0))
    return pl.pallas_call(
        functools.partial(_inproj_kernel, tiles_per_seq=tiles_per_seq),
        grid=(T // tm,),
        in_specs=[
            rows(D_MODEL),
            _const_spec((1, D_MODEL)),
            _const_spec(w_gdn.shape),
            _const_spec(w_ab.shape),
            _const_spec(w_diff.shape),
            _const_spec((GDN_CONV, 3 * GDN_QK)),
            _const_spec((8, LANES)),
        ],
        out_specs=[rows(3 * GDN_QK), rows(GDN_V), rows(LANES), rows(3 * DIFF_QK), rows(2 * D_MODEL)],
        out_shape=[
            jax.ShapeDtypeStruct((T, 3 * GDN_QK), BF16),
            jax.ShapeDtypeStruct((T, GDN_V), BF16),
            jax.ShapeDtypeStruct((T, LANES), F32),
            jax.ShapeDtypeStruct((T, 3 * DIFF_QK), BF16),
            jax.ShapeDtypeStruct((T, 2 * D_MODEL), BF16),
        ],
        scratch_shapes=[pltpu.VMEM((tm, D_MODEL), BF16), pltpu.VMEM((_HALO, 3 * GDN_QK), F32)],
        compiler_params=pltpu.CompilerParams(
            dimension_semantics=("arbitrary",), vmem_limit_bytes=VMEM_LIMIT),
        name="inproj",
    )(x2, g, w_gdn, w_ab, w_diff, conv_w, gparams)


def _split3_dot(x, sel):
    hi = x.astype(BF16)
    r1 = x - hi.astype(F32)
    mid = r1.astype(BF16)
    lo = (r1 - mid.astype(F32)).astype(BF16)
    return _dot(hi, sel) + _dot(mid, sel) + _dot(lo, sel)


def _pair(a, b):
    return jnp.concatenate([a, b], axis=1)


def _blockdiag2(x):
    c = x.shape[0]
    z = jnp.zeros((c, c), x.dtype)
    return jnp.concatenate([_pair(x[:, :c], z), _pair(z, x[:, c:])], axis=0)


def _gdn_stages(q_ref, k_ref, v_ref, z_ref, g_ref, beta_ref, onorm_ref, o_ref):
    S = q_ref.shape[1]
    C = GDN_CHUNK
    NC = S // C

    ii = lax.broadcasted_iota(jnp.int32, (C, C), 0)
    jj = lax.broadcasted_iota(jnp.int32, (C, C), 1)
    ii2 = _pair(ii, ii)
    jj2 = _pair(jj, jj)
    strict2 = ii2 > jj2
    incl2 = ii2 >= jj2
    eye2 = (ii2 == jj2).astype(F32)

    gc = _split3_dot(g_ref[0, 0], (ii <= jj).astype(BF16))
    beta = beta_ref[0, 0]

    tiles = []
    for c in range(NC):
        gc_row = gc[c:c + 1, :]
        rg = jnp.broadcast_to(gc_row, (C, C))
        cg = rg.T
        glast = cg[C - 1:C, :]
        kf = k_ref[0, c * C:(c + 1) * C, :].astype(F32)
        eg = jnp.exp(cg)
        tiles.append(dict(
            decay=jnp.exp(jnp.minimum(cg - rg, 0.0)),
            beta_row=beta[c:c + 1, :],
            k=k_ref[0, c * C:(c + 1) * C, :],
            q=q_ref[0, c * C:(c + 1) * C, :],
            k_eg=(kf * eg).astype(BF16),
            q_dec=q_ref[0, c * C:(c + 1) * C, :].astype(F32) * eg,
            kd_t=(kf.T * (jnp.exp(glast - gc_row) * beta[c:c + 1, :])).astype(BF16),
            gl=jnp.exp(glast),
        ))
        if c % 4 == 3:
            yield

    pairs = [(tiles[c0], tiles[c0 + 1]) for c0 in range(0, NC, 2)]
    b_mats, aqs = [], []
    for ta, tb in pairs:
        k2 = _pair(ta["k"], tb["k"])
        kq = _dot_nt(jnp.concatenate([k2, _pair(ta["q"], tb["q"])], axis=0), _blockdiag2(k2))
        decay2 = _pair(ta["decay"], tb["decay"])
        beta2 = jnp.broadcast_to(_pair(ta["beta_row"], tb["beta_row"]), (C, 2 * C))
        b_mats.append(jnp.where(strict2, kq[0:C, :] * decay2, 0.0) * beta2)
        aqs.append((jnp.where(incl2, kq[C:2 * C, :] * decay2, 0.0) * beta2).astype(BF16))
    yield

    ps = [eye2 - jnp.where((ii2 >> 1) == (jj2 >> 1), b, 0.0) for b in b_mats]
    lvl = 1
    while (1 << lvl) < C:
        in_parent = (ii2 >> (lvl + 1)) == (jj2 >> (lvl + 1))
        in_child = (ii2 >> lvl) == (jj2 >> lvl)
        e_mask = jnp.logical_and(in_parent, jnp.logical_not(in_child))
        pbs = [p.astype(BF16) for p in ps]
        pes = [_dot(pb, _blockdiag2(jnp.where(e_mask, b, 0.0).astype(BF16)))
               for pb, b in zip(pbs, b_mats)]
        yield
        ps = [p - _dot(pe.astype(BF16), _blockdiag2(pb)) for p, pe, pb in zip(ps, pes, pbs)]
        lvl += 1
        yield

    wus = []
    for pi, (p, (ta, tb)) in enumerate(zip(ps, pairs)):
        pb = p.astype(BF16)
        for half, t in enumerate((ta, tb)):
            c = 2 * pi + half
            vc = v_ref[0, c * C:(c + 1) * C, :]
            wus.append(_dot(pb[:, half * C:(half + 1) * C], _pair(t["k_eg"], vc)).astype(BF16))

    yield

    mns, qes, aus = [], [], []
    for c, (t, wu) in enumerate(zip(tiles, wus)):
        aq_c = aqs[c // 2][:, (c % 2) * C:(c % 2 + 1) * C]
        r2 = _dot(jnp.concatenate([t["kd_t"], aq_c], axis=0), wu)
        mns.append(r2[0:C, :])
        qes.append((t["q_dec"] - r2[C:2 * C, 0:GDN_DK]).astype(BF16))
        aus.append(r2[C:2 * C, GDN_DK:GDN_DK + GDN_DV])

    yield

    steps = []
    for pi, (ta, tb) in enumerate(pairs):
        mn_a, mn_b = mns[2 * pi], mns[2 * pi + 1]
        ga, gb = ta["gl"], tb["gl"]
        mb = mn_b[:, 0:GDN_DK].astype(BF16)
        x = _dot(mb, mn_a.astype(BF16))
        m2 = gb * mn_a[:, 0:GDN_DK] + ga * mn_b[:, 0:GDN_DK] - x[:, 0:GDN_DK]
        steps.append(dict(
            m=jnp.concatenate([m2.astype(BF16), mn_a[:, 0:GDN_DK].astype(BF16)], axis=0),
            n2=gb * mn_a[:, GDN_DK:] - x[:, GDN_DK:] + mn_b[:, GDN_DK:],
            na=mn_a[:, GDN_DK:], g2=ga * gb, ga=ga))

    yield

    def emit_out(c, sb):
        o = _dot(qes[c], sb) + aus[c]
        ms = jnp.mean(o * o, axis=-1, keepdims=True)
        o = o * lax.rsqrt(ms + NORM_EPS) * onorm_ref[...]
        rows = slice(c * C, (c + 1) * C)
        o_ref[0, rows, :] = (o * z_ref[0, rows, :].astype(F32)).astype(o_ref.dtype)

    state = jnp.zeros((GDN_DK, GDN_DV), F32)
    for pi, st in enumerate(steps):
        sb = state.astype(BF16)
        r = _dot(st["m"], sb)
        mid = state * st["ga"] - r[C:2 * C, :] + st["na"]
        emit_out(2 * pi, sb)
        emit_out(2 * pi + 1, mid.astype(BF16))
        state = state * st["g2"] - r[0:C, :] + st["n2"]
        yield


_ATT_BLOCK = 256
_LOG2E = math.log2(math.e)


def _diffattn_stages(lam_ref, subln_ref, q_ref, k_ref, v_ref, o_ref, s_scr, p_scr, vext_scr,
                     lambda_init):
    S = q_ref.shape[1]
    tq = min(_ATT_BLOCK, S)
    dv = 2 * DIFF_DH
    neg = jnp.finfo(F32).min

    lp = lam_ref[...]
    lam = (jnp.exp(jnp.sum(lp[0:1, :] * lp[1:2, :], axis=-1, keepdims=True))
           - jnp.exp(jnp.sum(lp[2:3, :] * lp[3:4, :], axis=-1, keepdims=True))
           + lambda_init)

    vext_scr[:, 0:dv] = v_ref[0]
    vext_scr[:, dv:2 * dv] = jnp.ones((S, dv), BF16)

    lane = lax.broadcasted_iota(jnp.int32, (tq, dv), 1)
    causal = (lax.broadcasted_iota(jnp.int32, (2 * tq, tq), 1)
              <= jnp.bitwise_and(lax.broadcasted_iota(jnp.int32, (2 * tq, tq), 0), tq - 1))

    for qb in reversed(range(S // tq)):
        r0 = qb * tq
        n = r0 + tq
        s_buf = s_scr.at[qb % 2]
        p_buf = p_scr.at[qb % 2]
        qs = (q_ref[0, r0:n, :].astype(F32) * (DIFF_DH ** -0.5 * _LOG2E)).astype(BF16)
        zero = jnp.zeros_like(qs)
        qq = jnp.concatenate([jnp.where(lane < DIFF_DH, qs, zero),
                              jnp.where(lane >= DIFF_DH, qs, zero)], axis=0)

        m_run = None
        for c0 in range(0, n, tq):
            s = _dot_nt(qq, k_ref[0, c0:c0 + tq, :])
            if c0 == r0:
                s = jnp.where(causal, s, neg)
            s_buf[c0 // tq] = s
            for t in range(0, tq, LANES):
                tile = s[:, t:t + LANES]
                m_run = tile if m_run is None else jnp.maximum(m_run, tile)
        m_b = jnp.broadcast_to(jnp.max(m_run, axis=-1, keepdims=True), (2 * tq, LANES))
        yield

        for c0 in range(0, n, tq):
            for t in range(0, tq, LANES):
                p_buf[:, c0 + t:c0 + t + LANES] = jnp.exp2(
                    s_buf[c0 // tq, :, t:t + LANES] - m_b).astype(BF16)
        yield

        oe = _dot(p_buf[:, 0:n], vext_scr[0:n, :])
        a = (oe[0:tq, 0:dv] / oe[0:tq, dv:2 * dv]
             - lam * (oe[tq:2 * tq, 0:dv] / oe[tq:2 * tq, dv:2 * dv]))
        ms = jnp.mean(a * a, axis=-1, keepdims=True)
        a = a * lax.rsqrt(ms + NORM_EPS) * subln_ref[...] * (1.0 - lambda_init)
        o_ref[0, r0:n, :] = a.astype(o_ref.dtype)
        yield


def _mixers_kernel(gq_ref, gk_ref, gv_ref, gz_ref, g_ref, beta_ref, onorm_ref,
                   lam_ref, subln_ref, dq_ref, dk_ref, dv_ref,
                   ogdn_ref, odiff_ref, s_scr, p_scr, vext_scr, *, lambda_init):
    attn = _diffattn_stages(lam_ref, subln_ref, dq_ref, dk_ref, dv_ref, odiff_ref,
                            s_scr, p_scr, vext_scr, lambda_init)
    gdn = _gdn_stages(gq_ref, gk_ref, gv_ref, gz_ref, g_ref, beta_ref, onorm_ref, ogdn_ref)
    live = {attn: 1, gdn: 2}
    while live:
        for st, count in list(live.items()):
            for _ in range(count):
                if st in live and next(st, _DONE) is _DONE:
                    del live[st]


_DONE = object()


def _mixers(qkv3, zs3, gbt, onorm, dqkv3, lam_params, subln, lambda_init):
    B, S, _ = qkv3.shape
    NC = S // GDN_CHUNK
    H = GDN_HEADS
    assert DIFF_HEADS == H
    tq = min(_ATT_BLOCK, S)
    dv = 2 * DIFF_DH
    head = lambda off: pl.BlockSpec((1, S, LANES), lambda b, h, off=off: (b, 0, off + h))
    gate = lambda off: pl.BlockSpec((1, 1, NC, GDN_CHUNK), lambda b, h, off=off: (b, off + h, 0, 0))
    return pl.pallas_call(
        functools.partial(_mixers_kernel, lambda_init=lambda_init),
        grid=(B, H),
        in_specs=[
            head(0), head(H), head(2 * H), head(0), gate(0), gate(H),
            pl.BlockSpec((1, LANES), lambda b, h: (0, 0)),
            pl.BlockSpec((8, DIFF_DH), lambda b, h: (0, 0)),
            pl.BlockSpec((1, dv), lambda b, h: (0, 0)),
            head(0), head(H), head(2 * H),
        ],
        out_specs=[head(0), head(0)],
        out_shape=[jax.ShapeDtypeStruct((B, S, GDN_V), BF16),
                   jax.ShapeDtypeStruct((B, S, DIFF_V), BF16)],
        scratch_shapes=[
            pltpu.VMEM((2, S // tq, 2 * tq, tq), F32),
            pltpu.VMEM((2, 2 * tq, S), BF16),
            pltpu.VMEM((S, 2 * dv), BF16),
        ],
        compiler_params=pltpu.CompilerParams(
            dimension_semantics=("arbitrary", "arbitrary"), vmem_limit_bytes=VMEM_LIMIT),
        name="mixers",
    )(qkv3, qkv3, qkv3, zs3, gbt, gbt, onorm, lam_params, subln, dqkv3, dqkv3, dqkv3)


def _merge_kernel(x_ref, og_ref, od_ref, gates_ref, wbg_ref, wbd_ref, wout_ref,
                  npost_ref, npre_ref, x1_ref, h2_ref):
    tm = x_ref.shape[0]
    half = tm // 2
    for r0 in (0, half):
        rs = slice(r0, r0 + half)
        mg = _dot(og_ref[rs, :], wbg_ref[...])
        md = _dot(od_ref[rs, :], wbd_ref[...])
        gg = _sigmoid(gates_ref[rs, 0:D_MODEL].astype(F32))
        gd = _sigmoid(gates_ref[rs, D_MODEL:2 * D_MODEL].astype(F32))
        merged = (gg * mg + gd * md).astype(BF16)
        m = _dot(merged, wout_ref[...])
        ms = jnp.mean(m * m, axis=-1, keepdims=True)
        x1 = x_ref[rs, :] + m * lax.rsqrt(ms + NORM_EPS) * npost_ref[...]
        x1_ref[rs, :] = x1
        ms1 = jnp.mean(x1 * x1, axis=-1, keepdims=True)
        h2_ref[rs, :] = (x1 * lax.rsqrt(ms1 + NORM_EPS) * npre_ref[...]).astype(BF16)


def _merge(x2, og, od, gates, wbg, wbd, wout, npost, npre, tm):
    T = x2.shape[0]
    row = lambda w: pl.BlockSpec((tm, w), lambda i: (i, 0))
    return pl.pallas_call(
        _merge_kernel,
        grid=(T // tm,),
        in_specs=[
            row(D_MODEL), row(GDN_V), row(DIFF_V), row(2 * D_MODEL),
            _const_spec((GDN_V, D_MODEL)), _const_spec((DIFF_V, D_MODEL)),
            _const_spec((D_MODEL, D_MODEL)),
            _const_spec((1, D_MODEL)), _const_spec((1, D_MODEL)),
        ],
        out_specs=[row(D_MODEL), row(D_MODEL)],
        out_shape=[jax.ShapeDtypeStruct((T, D_MODEL), F32),
                   jax.ShapeDtypeStruct((T, D_MODEL), BF16)],
        compiler_params=pltpu.CompilerParams(
            dimension_semantics=("arbitrary",), vmem_limit_bytes=VMEM_LIMIT),
        name="merge",
    )(x2, og, od, gates, wbg, wbd, wout, npost, npre)


_FFN_CHUNK = 512


def _ffn_kernel(x1_ref, h2_ref, wup_ref, cw_ref, cb_ref, wdown_ref, npost_ref, out_ref,
                carry, act_scr):
    si = pl.program_id(1)
    tm = h2_ref.shape[1]

    @pl.when(si == 0)
    def _():
        carry[...] = jnp.zeros_like(carry)

    h2 = h2_ref[0]

    def conv_half(col0, cw):
        u = _dot(h2, wup_ref[:, col0:col0 + cw])
        prev = carry[:, col0:col0 + cw]
        carry[:, col0:col0 + cw] = u[tm - _HALO:tm, :]
        return _causal_conv(u, prev, cw_ref[:, col0:col0 + cw], FFN_CONV) + cb_ref[:, col0:col0 + cw]

    for c0 in range(0, D_FF, _FFN_CHUNK):
        cw = min(_FFN_CHUNK, D_FF - c0)
        gate = conv_half(c0, cw)
        up = conv_half(D_FF + c0, cw)
        act_scr[:, c0:c0 + cw] = (_silu(gate) * up).astype(BF16)

    f = _dot(act_scr[...], wdown_ref[...])
    ms = jnp.mean(f * f, axis=-1, keepdims=True)
    out_ref[0] = x1_ref[0] + f * lax.rsqrt(ms + NORM_EPS) * npost_ref[...]


def _ffn(x1_3, h2_3, wup, cw, cb, wdown, npost, tm):
    B, S, _ = x1_3.shape
    seq = lambda: pl.BlockSpec((1, tm, D_MODEL), lambda b, s: (b, s, 0))
    return pl.pallas_call(
        _ffn_kernel,
        grid=(B, S // tm),
        in_specs=[
            seq(), seq(),
            _const_spec((D_MODEL, 2 * D_FF)),
            _const_spec((FFN_CONV, 2 * D_FF)),
            _const_spec((1, 2 * D_FF)),
            _const_spec((D_FF, D_MODEL)),
            _const_spec((1, D_MODEL)),
        ],
        out_specs=seq(),
        out_shape=jax.ShapeDtypeStruct((B, S, D_MODEL), F32),
        scratch_shapes=[
            pltpu.VMEM((_HALO, 2 * D_FF), F32),
            pltpu.VMEM((tm, D_FF), BF16),
        ],
        compiler_params=pltpu.CompilerParams(
            dimension_semantics=("arbitrary", "arbitrary"), vmem_limit_bytes=VMEM_LIMIT),
        name="convffn",
    )(x1_3, h2_3, wup, cw, cb, wdown, npost)


def _row_tile(n, want):
    t = min(want, n)
    while n % t:
        t //= 2
    return t


def _layer(x, lambda_init, norm_mix_pre, w_in, conv_qkv_w, gdn_A_log, gdn_dt_bias, gdn_out_norm,
           lambda_q1, lambda_k1, lambda_q2, lambda_k2, diff_subln, w_branch_gdn, w_branch_diff,
           w_out, norm_mix_post, norm_ffn_pre, w_up, ffn_conv_w, ffn_conv_b, w_down, norm_ffn_post):
    B, S, D = x.shape
    T = B * S
    x2 = x.reshape(T, D)
    row2 = lambda v: v.reshape(1, -1).astype(F32)

    n_gdn = 4 * GDN_QK
    w_gdn = w_in[:, :n_gdn].astype(BF16)
    w_ab = jnp.pad(w_in[:, n_gdn:n_gdn + 2 * GDN_HEADS].astype(BF16),
                   ((0, 0), (0, LANES - 2 * GDN_HEADS)))
    w_diff = w_in[:, n_gdn + 2 * GDN_HEADS:].astype(BF16)
    gparams = jnp.zeros((8, LANES), F32)
    gparams = gparams.at[0, :GDN_HEADS].set(gdn_A_log.astype(F32))
    gparams = gparams.at[1, :GDN_HEADS].set(gdn_dt_bias.astype(F32))

    tm = _row_tile(S, 512)
    qkv, zs, gb, dqkv, gates = _inproj(x2, row2(norm_mix_pre), w_gdn, w_ab, w_diff,
                                       conv_qkv_w.astype(F32), gparams, tm, S // tm)

    gbt = jnp.transpose(gb.reshape(B, S, LANES)[:, :, :2 * GDN_HEADS], (0, 2, 1))
    gbt = gbt.reshape(B, 2 * GDN_HEADS, S // GDN_CHUNK, GDN_CHUNK)
    lam_params = jnp.zeros((8, DIFF_DH), F32)
    lam_params = lam_params.at[0].set(lambda_q1.astype(F32)).at[1].set(lambda_k1.astype(F32))
    lam_params = lam_params.at[2].set(lambda_q2.astype(F32)).at[3].set(lambda_k2.astype(F32))
    o_gdn, o_diff = _mixers(qkv.reshape(B, S, -1), zs.reshape(B, S, -1), gbt, row2(gdn_out_norm),
                            dqkv.reshape(B, S, -1), lam_params, row2(diff_subln), lambda_init)

    x1, h2 = _merge(x2, o_gdn.reshape(T, -1), o_diff.reshape(T, -1), gates,
                    w_branch_gdn.astype(BF16), w_branch_diff.astype(BF16), w_out.astype(BF16),
                    row2(norm_mix_post), row2(norm_ffn_pre), _row_tile(T, 1024))

    out = _ffn(x1.reshape(B, S, D), h2.reshape(B, S, D), w_up.astype(BF16),
               ffn_conv_w.astype(F32), row2(ffn_conv_b), w_down.astype(BF16),
               row2(norm_ffn_post), _row_tile(S, 512))
    return out


def kernel(x, norm_mix_pre, w_in, conv_qkv_w, gdn_A_log, gdn_dt_bias, gdn_out_norm, lambda_q1, lambda_k1, lambda_q2, lambda_k2, diff_subln, w_branch_gdn, w_branch_diff, w_out, norm_mix_post, norm_ffn_pre, w_up, ffn_conv_w, ffn_conv_b, w_down, norm_ffn_post):
    depth = w_in.shape[0]
    for l in range(depth):
        lambda_init = 0.8 - 0.6 * math.exp(-0.3 * l)
        x = _layer(x, lambda_init, norm_mix_pre[l], w_in[l], conv_qkv_w[l], gdn_A_log[l],
                   gdn_dt_bias[l], gdn_out_norm[l], lambda_q1[l], lambda_k1[l], lambda_q2[l],
                   lambda_k2[l], diff_subln[l], w_branch_gdn[l], w_branch_diff[l], w_out[l],
                   norm_mix_post[l], norm_ffn_pre[l], w_up[l], ffn_conv_w[l], ffn_conv_b[l],
                   w_down[l], norm_ffn_post[l])
    return x
```

```python
import functools
import math

import jax
import jax.numpy as jnp
from jax import lax
from jax.experimental import pallas as pl
from jax.experimental.pallas import tpu as pltpu

F32 = jnp.float32
BF16 = jnp.bfloat16

D_MODEL = 1024
GDN_HEADS = 8
GDN_DK = 128
GDN_DV = 128
GDN_CONV = 4
DIFF_HEADS = 8
DIFF_DH = 64
D_FF = 2816
FFN_CONV = 3
NORM_EPS = 1e-6

GDN_QK = GDN_HEADS * GDN_DK
GDN_V = GDN_HEADS * GDN_DV
DIFF_QK = DIFF_HEADS * 2 * DIFF_DH
DIFF_V = DIFF_HEADS * 2 * DIFF_DH

LANES = 128
GDN_CHUNK = 128
VMEM_LIMIT = 56 * 1024 * 1024


def _dot(a, b):
    return jnp.dot(a, b, preferred_element_type=F32)


def _dot_nt(a, b):
    return lax.dot_general(a, b, (((1,), (1,)), ((), ())), preferred_element_type=F32)


def _bdot(a, b):
    return lax.dot_general(a, b, (((2,), (1,)), ((0,), (0,))), preferred_element_type=F32)


def _sigmoid(x):
    return 1.0 / (1.0 + jnp.exp(-x))


def _silu(x):
    hx = 0.5 * x
    return hx * jnp.tanh(hx) + hx


def _softplus(x):
    return jnp.maximum(x, 0.0) + jnp.log1p(jnp.exp(-jnp.abs(x)))


def _const_spec(shape):
    nd = len(shape)
    return pl.BlockSpec(shape, lambda *_: (0,) * nd, pipeline_mode=pl.Buffered(1))


_HALO = 8


def _causal_conv(u, prev, w, taps):
    tm, cw = u.shape
    row = lax.broadcasted_iota(jnp.int32, prev.shape, 0)
    u3 = u.reshape(tm // _HALO, _HALO, cw)
    y = u3 * w[taps - 1:taps, :]
    for sh in range(1, taps):
        r = pltpu.roll(u3, sh, 1)
        before = jnp.concatenate([pltpu.roll(prev, sh, 0)[None], r[:-1]], axis=0)
        y = y + jnp.where(row < sh, before, r) * w[taps - 1 - sh:taps - sh, :]
    return y.reshape(tm, cw)


_W_BLOCK = 512


def _cast_gdn_kernel(w_ref, ab_ref, wg_ref, wab_ref):
    wg_ref[...] = w_ref[...].astype(BF16)
    lane = lax.broadcasted_iota(jnp.int32, ab_ref.shape, 1)
    wab_ref[...] = jnp.where(lane < 2 * GDN_HEADS, ab_ref[...], 0.0).astype(BF16)


def _cast_diff_kernel(a_ref, b_ref, o_ref):
    off = 2 * GDN_HEADS
    x = jnp.concatenate([a_ref[...], b_ref[...]], axis=1)
    o_ref[...] = x[:, off:off + _W_BLOCK].astype(BF16)


def _cast_in_weights(w_in):
    n_gdn = 4 * GDN_QK
    n_diff = 3 * DIFF_QK + 2 * D_MODEL
    assert w_in.shape == (D_MODEL, n_gdn + 2 * GDN_HEADS + n_diff)
    col = lambda width, f: pl.BlockSpec((D_MODEL, width), f)
    w_gdn, w_ab = pl.pallas_call(
        _cast_gdn_kernel,
        grid=(n_gdn // _W_BLOCK,),
        in_specs=[col(_W_BLOCK, lambda j: (0, j)), col(LANES, lambda j: (0, n_gdn // LANES))],
        out_specs=[col(_W_BLOCK, lambda j: (0, j)), col(LANES, lambda j: (0, 0))],
        out_shape=[jax.ShapeDtypeStruct((D_MODEL, n_gdn), BF16),
                   jax.ShapeDtypeStruct((D_MODEL, LANES), BF16)],
        compiler_params=pltpu.CompilerParams(dimension_semantics=("arbitrary",)),
        name="cast_w_gdn",
    )(w_in, w_in)
    w_diff = pl.pallas_call(
        _cast_diff_kernel,
        grid=(n_diff // _W_BLOCK,),
        in_specs=[col(_W_BLOCK, lambda j: (0, n_gdn // _W_BLOCK + j)),
                  col(LANES, lambda j: (0, (n_gdn + _W_BLOCK) // LANES + (_W_BLOCK // LANES) * j))],
        out_specs=col(_W_BLOCK, lambda j: (0, j)),
        out_shape=jax.ShapeDtypeStruct((D_MODEL, n_diff), BF16),
        compiler_params=pltpu.CompilerParams(dimension_semantics=("arbitrary",)),
        name="cast_w_diff",
    )(w_in, w_in)
    return w_gdn, w_ab, w_diff


_IN_CHUNK = 512


def _inproj_kernel(x_ref, g_ref, wg_ref, wab_ref, wd_ref, cw_ref, gp_ref,
                   qkv_ref, z_ref, gb_ref, dqkv_ref, gates_ref, h_scr, carry, *, tiles_per_seq):
    tm = x_ref.shape[0]

    @pl.when(pl.program_id(0) % tiles_per_seq == 0)
    def _():
        carry[...] = jnp.zeros_like(carry)

    x = x_ref[...]
    ms = jnp.mean(x * x, axis=-1, keepdims=True)
    h_scr[...] = (x * lax.rsqrt(ms + NORM_EPS) * g_ref[...]).astype(BF16)

    plain = ([(dqkv_ref, c0, c0) for c0 in range(0, 3 * DIFF_QK, _IN_CHUNK)]
             + [(gates_ref, c0, 3 * DIFF_QK + c0) for c0 in range(0, 2 * D_MODEL, _IN_CHUNK)])

    def emit_plain(count):
        for _ in range(min(count, len(plain))):
            out_ref, c0, w_off = plain.pop(0)
            acc = _dot(h_scr[...], wd_ref[:, w_off:w_off + _IN_CHUNK])
            out_ref[:, c0:c0 + _IN_CHUNK] = acc.astype(out_ref.dtype)

    for c0 in range(0, 3 * GDN_QK, _IN_CHUNK):
        emit_plain(1)
        u = _dot(h_scr[...], wg_ref[:, c0:c0 + _IN_CHUNK])
        prev = carry[:, c0:c0 + _IN_CHUNK]
        carry[:, c0:c0 + _IN_CHUNK] = u[tm - _HALO:tm, :]
        y = _silu(_causal_conv(u, prev, cw_ref[:, c0:c0 + _IN_CHUNK], GDN_CONV))
        if c0 < 2 * GDN_QK:
            scale = GDN_DK ** -0.5 if c0 < GDN_QK else 1.0
            heads = []
            for h0 in range(0, _IN_CHUNK, GDN_DK):
                yh = y[:, h0:h0 + GDN_DK]
                inv = lax.rsqrt(jnp.sum(yh * yh, axis=-1, keepdims=True) + NORM_EPS)
                heads.append(yh * (inv * scale))
            y = jnp.concatenate(heads, axis=1)
        qkv_ref[:, c0:c0 + _IN_CHUNK] = y.astype(BF16)

    for c0 in range(0, GDN_V, _IN_CHUNK):
        emit_plain(1)
        zc = _dot(h_scr[...], wg_ref[:, 3 * GDN_QK + c0:3 * GDN_QK + c0 + _IN_CHUNK])
        z_ref[:, c0:c0 + _IN_CHUNK] = _silu(zc).astype(BF16)

    ab = _dot(h_scr[...], wab_ref[...])
    gp = gp_ref[...]
    g = -jnp.exp(gp[0:1, :]) * _softplus(ab + gp[1:2, :])
    lane = lax.broadcasted_iota(jnp.int32, ab.shape, 1)
    gb_ref[...] = jnp.where(lane < GDN_HEADS, g, _sigmoid(ab))
    emit_plain(len(plain))


def _inproj(x2, g, w_gdn, w_ab, w_diff, conv_w, gparams, tm, tiles_per_seq):
    T = x2.shape[0]
    rows = lambda w: pl.BlockSpec((tm, w), lambda i: (i, 0))
    return pl.pallas_call(
        functools.partial(_inproj_kernel, tiles_per_seq=tiles_per_seq),
        grid=(T // tm,),
        in_specs=[
            rows(D_MODEL),
            _const_spec((1, D_MODEL)),
            _const_spec(w_gdn.shape),
            _const_spec(w_ab.shape),
            _const_spec(w_diff.shape),
            _const_spec((GDN_CONV, 3 * GDN_QK)),
            _const_spec((8, LANES)),
        ],
        out_specs=[rows(3 * GDN_QK), rows(GDN_V), rows(LANES), rows(3 * DIFF_QK), rows(2 * D_MODEL)],
        out_shape=[
            jax.ShapeDtypeStruct((T, 3 * GDN_QK), BF16),
            jax.ShapeDtypeStruct((T, GDN_V), BF16),
            jax.ShapeDtypeStruct((T, LANES), F32),
            jax.ShapeDtypeStruct((T, 3 * DIFF_QK), BF16),
            jax.ShapeDtypeStruct((T, 2 * D_MODEL), BF16),
        ],
        scratch_shapes=[pltpu.VMEM((tm, D_MODEL), BF16), pltpu.VMEM((_HALO, 3 * GDN_QK), F32)],
        compiler_params=pltpu.CompilerParams(
            dimension_semantics=("arbitrary",), vmem_limit_bytes=VMEM_LIMIT),
        name="inproj",
    )(x2, g, w_gdn, w_ab, w_diff, conv_w, gparams)


def _split3_dot(x, sel):
    hi = x.astype(BF16)
    r1 = x - hi.astype(F32)
    mid = r1.astype(BF16)
    lo = (r1 - mid.astype(F32)).astype(BF16)
    return _dot(hi, sel) + _dot(mid, sel) + _dot(lo, sel)


def _pair(a, b):
    return jnp.concatenate([a, b], axis=1)


def _blockdiag2(x):
    c = x.shape[0]
    z = jnp.zeros((c, c), x.dtype)
    return jnp.concatenate([_pair(x[:, :c], z), _pair(z, x[:, c:])], axis=0)


def _gdn_stages(q_ref, k_ref, v_ref, z_ref, g_ref, beta_ref, onorm_ref, o_ref):
    S = q_ref.shape[1]
    C = GDN_CHUNK
    NC = S // C

    ii = lax.broadcasted_iota(jnp.int32, (C, C), 0)
    jj = lax.broadcasted_iota(jnp.int32, (C, C), 1)
    ii2 = _pair(ii, ii)
    jj2 = _pair(jj, jj)
    strict2 = ii2 > jj2
    incl2 = ii2 >= jj2
    eye2 = (ii2 == jj2).astype(F32)

    gc = _split3_dot(g_ref[0, 0], (ii <= jj).astype(BF16))
    beta = beta_ref[0, 0]

    tiles = []
    for c in range(NC):
        gc_row = gc[c:c + 1, :]
        rg = jnp.broadcast_to(gc_row, (C, C))
        cg = rg.T
        glast = cg[C - 1:C, :]
        kf = k_ref[0, c * C:(c + 1) * C, :].astype(F32)
        eg = jnp.exp(cg)
        tiles.append(dict(
            decay=jnp.exp(jnp.minimum(cg - rg, 0.0)),
            beta_row=beta[c:c + 1, :],
            k=k_ref[0, c * C:(c + 1) * C, :],
            q=q_ref[0, c * C:(c + 1) * C, :],
            k_eg=(kf * eg).astype(BF16),
            q_dec=q_ref[0, c * C:(c + 1) * C, :].astype(F32) * eg,
            kd_t=(kf.T * (jnp.exp(glast - gc_row) * beta[c:c + 1, :])).astype(BF16),
            gl=jnp.exp(glast),
        ))
        if c % 4 == 3:
            yield

    pairs = [(tiles[c0], tiles[c0 + 1]) for c0 in range(0, NC, 2)]
    b_mats, aqs = [], []
    for ta, tb in pairs:
        k2 = _pair(ta["k"], tb["k"])
        kq = _dot_nt(jnp.concatenate([k2, _pair(ta["q"], tb["q"])], axis=0), _blockdiag2(k2))
        decay2 = _pair(ta["decay"], tb["decay"])
        beta2 = jnp.broadcast_to(_pair(ta["beta_row"], tb["beta_row"]), (C, 2 * C))
        b_mats.append(jnp.where(strict2, kq[0:C, :] * decay2, 0.0) * beta2)
        aqs.append((jnp.where(incl2, kq[C:2 * C, :] * decay2, 0.0) * beta2).astype(BF16))
    yield

    ps = [eye2 - jnp.where((ii2 >> 1) == (jj2 >> 1), b, 0.0) for b in b_mats]
    lvl = 1
    while (1 << lvl) < C:
        in_parent = (ii2 >> (lvl + 1)) == (jj2 >> (lvl + 1))
        in_child = (ii2 >> lvl) == (jj2 >> lvl)
        e_mask = jnp.logical_and(in_parent, jnp.logical_not(in_child))
        pbs = [p.astype(BF16) for p in ps]
        pes = [_dot(pb, _blockdiag2(jnp.where(e_mask, b, 0.0).astype(BF16)))
               for pb, b in zip(pbs, b_mats)]
        yield
        ps = [p - _dot(pe.astype(BF16), _blockdiag2(pb)) for p, pe, pb in zip(ps, pes, pbs)]
        lvl += 1
        yield

    wus = []
    for pi, (p, (ta, tb)) in enumerate(zip(ps, pairs)):
        pb = p.astype(BF16)
        for half, t in enumerate((ta, tb)):
            c = 2 * pi + half
            vc = v_ref[0, c * C:(c + 1) * C, :]
            wus.append(_dot(pb[:, half * C:(half + 1) * C], _pair(t["k_eg"], vc)).astype(BF16))

    yield

    mns, qes, aus = [], [], []
    for c, (t, wu) in enumerate(zip(tiles, wus)):
        aq_c = aqs[c // 2][:, (c % 2) * C:(c % 2 + 1) * C]
        r2 = _dot(jnp.concatenate([t["kd_t"], aq_c], axis=0), wu)
        mns.append(r2[0:C, :])
        qes.append((t["q_dec"] - r2[C:2 * C, 0:GDN_DK]).astype(BF16))
        aus.append(r2[C:2 * C, GDN_DK:GDN_DK + GDN_DV])

    yield

    steps = []
    for pi, (ta, tb) in enumerate(pairs):
        mn_a, mn_b = mns[2 * pi], mns[2 * pi + 1]
        ga, gb = ta["gl"], tb["gl"]
        mb = mn_b[:, 0:GDN_DK].astype(BF16)
        x = _dot(mb, mn_a.astype(BF16))
        m2 = gb * mn_a[:, 0:GDN_DK] + ga * mn_b[:, 0:GDN_DK] - x[:, 0:GDN_DK]
        steps.append(dict(
            m=jnp.concatenate([m2.astype(BF16), mn_a[:, 0:GDN_DK].astype(BF16)], axis=0),
            n2=gb * mn_a[:, GDN_DK:] - x[:, GDN_DK:] + mn_b[:, GDN_DK:],
            na=mn_a[:, GDN_DK:], g2=ga * gb, ga=ga))

    yield

    def emit_out(c, sb):
        o = _dot(qes[c], sb) + aus[c]
        ms = jnp.mean(o * o, axis=-1, keepdims=True)
        o = o * lax.rsqrt(ms + NORM_EPS) * onorm_ref[...]
        rows = slice(c * C, (c + 1) * C)
        o_ref[0, rows, :] = (o * z_ref[0, rows, :].astype(F32)).astype(o_ref.dtype)

    state = jnp.zeros((GDN_DK, GDN_DV), F32)
    for pi, st in enumerate(steps):
        sb = state.astype(BF16)
        r = _dot(st["m"], sb)
        mid = state * st["ga"] - r[C:2 * C, :] + st["na"]
        emit_out(2 * pi, sb)
        emit_out(2 * pi + 1, mid.astype(BF16))
        state = state * st["g2"] - r[0:C, :] + st["n2"]
        yield


_ATT_BLOCK = 256
_LOG2E = math.log2(math.e)


def _diffattn_stages(lam_ref, subln_ref, q_ref, k_ref, v_ref, o_ref, s_scr, p_scr, vext_scr,
                     lambda_init):
    S = q_ref.shape[1]
    tq = min(_ATT_BLOCK, S)
    dv = 2 * DIFF_DH
    neg = jnp.finfo(F32).min

    lp = lam_ref[...]
    lam = (jnp.exp(jnp.sum(lp[0:1, :] * lp[1:2, :], axis=-1, keepdims=True))
           - jnp.exp(jnp.sum(lp[2:3, :] * lp[3:4, :], axis=-1, keepdims=True))
           + lambda_init)

    vext_scr[:, 0:dv] = v_ref[0]
    vext_scr[:, dv:2 * dv] = jnp.ones((S, dv), BF16)

    lane = lax.broadcasted_iota(jnp.int32, (tq, dv), 1)
    causal = (lax.broadcasted_iota(jnp.int32, (2 * tq, tq), 1)
              <= jnp.bitwise_and(lax.broadcasted_iota(jnp.int32, (2 * tq, tq), 0), tq - 1))

    for qb in reversed(range(S // tq)):
        r0 = qb * tq
        n = r0 + tq
        s_buf = s_scr.at[qb % 2]
        p_buf = p_scr.at[qb % 2]
        qs = (q_ref[0, r0:n, :].astype(F32) * (DIFF_DH ** -0.5 * _LOG2E)).astype(BF16)
        zero = jnp.zeros_like(qs)
        qq = jnp.concatenate([jnp.where(lane < DIFF_DH, qs, zero),
                              jnp.where(lane >= DIFF_DH, qs, zero)], axis=0)

        m_run = None
        for c0 in range(0, n, tq):
            s = _dot_nt(qq, k_ref[0, c0:c0 + tq, :])
            if c0 == r0:
                s = jnp.where(causal, s, neg)
            s_buf[c0 // tq] = s
            for t in range(0, tq, LANES):
                tile = s[:, t:t + LANES]
                m_run = tile if m_run is None else jnp.maximum(m_run, tile)
        m_b = jnp.broadcast_to(jnp.max(m_run, axis=-1, keepdims=True), (2 * tq, LANES))
        yield

        for c0 in range(0, n, tq):
            for t in range(0, tq, LANES):
                p_buf[:, c0 + t:c0 + t + LANES] = jnp.exp2(
                    s_buf[c0 // tq, :, t:t + LANES] - m_b).astype(BF16)
        yield

        oe = _dot(p_buf[:, 0:n], vext_scr[0:n, :])
        a = (oe[0:tq, 0:dv] / oe[0:tq, dv:2 * dv]
             - lam * (oe[tq:2 * tq, 0:dv] / oe[tq:2 * tq, dv:2 * dv]))
        ms = jnp.mean(a * a, axis=-1, keepdims=True)
        a = a * lax.rsqrt(ms + NORM_EPS) * subln_ref[...] * (1.0 - lambda_init)
        o_ref[0, r0:n, :] = a.astype(o_ref.dtype)
        yield


def _mixers_kernel(gq_ref, gk_ref, gv_ref, gz_ref, g_ref, beta_ref, onorm_ref,
                   lam_ref, subln_ref, dq_ref, dk_ref, dv_ref,
                   ogdn_ref, odiff_ref, s_scr, p_scr, vext_scr, *, lambda_init):
    attn = _diffattn_stages(lam_ref, subln_ref, dq_ref, dk_ref, dv_ref, odiff_ref,
                            s_scr, p_scr, vext_scr, lambda_init)
    gdn = _gdn_stages(gq_ref, gk_ref, gv_ref, gz_ref, g_ref, beta_ref, onorm_ref, ogdn_ref)
    live = {attn: 1, gdn: 2}
    while live:
        for st, count in list(live.items()):
            for _ in range(count):
                if st in live and next(st, _DONE) is _DONE:
                    del live[st]


_DONE = object()


def _mixers(qkv3, zs3, gbt, onorm, dqkv3, lam_params, subln, lambda_init):
    B, S, _ = qkv3.shape
    NC = S // GDN_CHUNK
    H = GDN_HEADS
    assert DIFF_HEADS == H
    tq = min(_ATT_BLOCK, S)
    dv = 2 * DIFF_DH
    head = lambda off: pl.BlockSpec((1, S, LANES), lambda b, h, off=off: (b, 0, off + h))
    gate = lambda off: pl.BlockSpec((1, 1, NC, GDN_CHUNK), lambda b, h, off=off: (b, off + h, 0, 0))
    return pl.pallas_call(
        functools.partial(_mixers_kernel, lambda_init=lambda_init),
        grid=(B, H),
        in_specs=[
            head(0), head(H), head(2 * H), head(0), gate(0), gate(H),
            pl.BlockSpec((1, LANES), lambda b, h: (0, 0)),
            pl.BlockSpec((8, DIFF_DH), lambda b, h: (0, 0)),
            pl.BlockSpec((1, dv), lambda b, h: (0, 0)),
            head(0), head(H), head(2 * H),
        ],
        out_specs=[head(0), head(0)],
        out_shape=[jax.ShapeDtypeStruct((B, S, GDN_V), BF16),
                   jax.ShapeDtypeStruct((B, S, DIFF_V), BF16)],
        scratch_shapes=[
            pltpu.VMEM((2, S // tq, 2 * tq, tq), F32),
            pltpu.VMEM((2, 2 * tq, S), BF16),
            pltpu.VMEM((S, 2 * dv), BF16),
        ],
        compiler_params=pltpu.CompilerParams(
            dimension_semantics=("arbitrary", "arbitrary"), vmem_limit_bytes=VMEM_LIMIT),
        name="mixers",
    )(qkv3, qkv3, qkv3, zs3, gbt, gbt, onorm, lam_params, subln, dqkv3, dqkv3, dqkv3)


def _merge_kernel(x_ref, og_ref, od_ref, gates_ref, wbg_ref, wbd_ref, wout_ref,
                  npost_ref, npre_ref, x1_ref, h2_ref):
    tm = x_ref.shape[0]
    half = tm // 2
    for r0 in (0, half):
        rs = slice(r0, r0 + half)
        mg = _dot(og_ref[rs, :], wbg_ref[...])
        md = _dot(od_ref[rs, :], wbd_ref[...])
        gg = _sigmoid(gates_ref[rs, 0:D_MODEL].astype(F32))
        gd = _sigmoid(gates_ref[rs, D_MODEL:2 * D_MODEL].astype(F32))
        merged = (gg * mg + gd * md).astype(BF16)
        m = _dot(merged, wout_ref[...])
        ms = jnp.mean(m * m, axis=-1, keepdims=True)
        x1 = x_ref[rs, :] + m * lax.rsqrt(ms + NORM_EPS) * npost_ref[...]
        x1_ref[rs, :] = x1
        ms1 = jnp.mean(x1 * x1, axis=-1, keepdims=True)
        h2_ref[rs, :] = (x1 * lax.rsqrt(ms1 + NORM_EPS) * npre_ref[...]).astype(BF16)


def _merge(x2, og, od, gates, wbg, wbd, wout, npost, npre, tm):
    T = x2.shape[0]
    row = lambda w: pl.BlockSpec((tm, w), lambda i: (i, 0))
    return pl.pallas_call(
        _merge_kernel,
        grid=(T // tm,),
        in_specs=[
            row(D_MODEL), row(GDN_V), row(DIFF_V), row(2 * D_MODEL),
            _const_spec((GDN_V, D_MODEL)), _const_spec((DIFF_V, D_MODEL)),
            _const_spec((D_MODEL, D_MODEL)),
            _const_spec((1, D_MODEL)), _const_spec((1, D_MODEL)),
        ],
        out_specs=[row(D_MODEL), row(D_MODEL)],
        out_shape=[jax.ShapeDtypeStruct((T, D_MODEL), F32),
                   jax.ShapeDtypeStruct((T, D_MODEL), BF16)],
        compiler_params=pltpu.CompilerParams(
            dimension_semantics=("arbitrary",), vmem_limit_bytes=VMEM_LIMIT),
        name="merge",
    )(x2, og, od, gates, wbg, wbd, wout, npost, npre)


_FFN_CHUNK = 512


def _ffn_kernel(x1_ref, h2_ref, wup_ref, cw_ref, cb_ref, wdown_ref, npost_ref, out_ref,
                carry, act_scr):
    si = pl.program_id(1)
    tm = h2_ref.shape[1]

    @pl.when(si == 0)
    def _():
        carry[...] = jnp.zeros_like(carry)

    h2 = h2_ref[0]

    def conv_half(col0, cw):
        u = _dot(h2, wup_ref[:, col0:col0 + cw])
        prev = carry[:, col0:col0 + cw]
        carry[:, col0:col0 + cw] = u[tm - _HALO:tm, :]
        return _causal_conv(u, prev, cw_ref[:, col0:col0 + cw], FFN_CONV) + cb_ref[:, col0:col0 + cw]

    for c0 in range(0, D_FF, _FFN_CHUNK):
        cw = min(_FFN_CHUNK, D_FF - c0)
        gate = conv_half(c0, cw)
        up = conv_half(D_FF + c0, cw)
        act_scr[:, c0:c0 + cw] = (_silu(gate) * up).astype(BF16)

    f = _dot(act_scr[...], wdown_ref[...])
    ms = jnp.mean(f * f, axis=-1, keepdims=True)
    out_ref[0] = x1_ref[0] + f * lax.rsqrt(ms + NORM_EPS) * npost_ref[...]


def _ffn(x1_3, h2_3, wup, cw, cb, wdown, npost, tm):
    B, S, _ = x1_3.shape
    seq = lambda: pl.BlockSpec((1, tm, D_MODEL), lambda b, s: (b, s, 0))
    return pl.pallas_call(
        _ffn_kernel,
        grid=(B, S // tm),
        in_specs=[
            seq(), seq(),
            _const_spec((D_MODEL, 2 * D_FF)),
            _const_spec((FFN_CONV, 2 * D_FF)),
            _const_spec((1, 2 * D_FF)),
            _const_spec((D_FF, D_MODEL)),
            _const_spec((1, D_MODEL)),
        ],
        out_specs=seq(),
        out_shape=jax.ShapeDtypeStruct((B, S, D_MODEL), F32),
        scratch_shapes=[
            pltpu.VMEM((_HALO, 2 * D_FF), F32),
            pltpu.VMEM((tm, D_FF), BF16),
        ],
        compiler_params=pltpu.CompilerParams(
            dimension_semantics=("arbitrary", "arbitrary"), vmem_limit_bytes=VMEM_LIMIT),
        name="convffn",
    )(x1_3, h2_3, wup, cw, cb, wdown, npost)


def _row_tile(n, want):
    t = min(want, n)
    while n % t:
        t //= 2
    return t


def _layer(x, lambda_init, norm_mix_pre, w_in, conv_qkv_w, gdn_A_log, gdn_dt_bias, gdn_out_norm,
           lambda_q1, lambda_k1, lambda_q2, lambda_k2, diff_subln, w_branch_gdn, w_branch_diff,
           w_out, norm_mix_post, norm_ffn_pre, w_up, ffn_conv_w, ffn_conv_b, w_down, norm_ffn_post):
    B, S, D = x.shape
    T = B * S
    x2 = x.reshape(T, D)
    row2 = lambda v: v.reshape(1, -1).astype(F32)

    w_gdn, w_ab, w_diff = _cast_in_weights(w_in)
    gparams = jnp.zeros((8, LANES), F32)
    gparams = gparams.at[0, :GDN_HEADS].set(gdn_A_log.astype(F32))
    gparams = gparams.at[1, :GDN_HEADS].set(gdn_dt_bias.astype(F32))

    tm = _row_tile(S, 512)
    qkv, zs, gb, dqkv, gates = _inproj(x2, row2(norm_mix_pre), w_gdn, w_ab, w_diff,
                                       conv_qkv_w.astype(F32), gparams, tm, S // tm)

    gbt = jnp.transpose(gb.reshape(B, S, LANES)[:, :, :2 * GDN_HEADS], (0, 2, 1))
    gbt = gbt.reshape(B, 2 * GDN_HEADS, S // GDN_CHUNK, GDN_CHUNK)
    lam_params = jnp.zeros((8, DIFF_DH), F32)
    lam_params = lam_params.at[0].set(lambda_q1.astype(F32)).at[1].set(lambda_k1.astype(F32))
    lam_params = lam_params.at[2].set(lambda_q2.astype(F32)).at[3].set(lambda_k2.astype(F32))
    o_gdn, o_diff = _mixers(qkv.reshape(B, S, -1), zs.reshape(B, S, -1), gbt, row2(gdn_out_norm),
                            dqkv.reshape(B, S, -1), lam_params, row2(diff_subln), lambda_init)

    x1, h2 = _merge(x2, o_gdn.reshape(T, -1), o_diff.reshape(T, -1), gates,
                    w_branch_gdn.astype(BF16), w_branch_diff.astype(BF16), w_out.astype(BF16),
                    row2(norm_mix_post), row2(norm_ffn_pre), _row_tile(T, 1024))

    out = _ffn(x1.reshape(B, S, D), h2.reshape(B, S, D), w_up.astype(BF16),
               ffn_conv_w.astype(F32), row2(ffn_conv_b), w_down.astype(BF16),
               row2(norm_ffn_post), _row_tile(S, 512))
    return out


def kernel(x, norm_mix_pre, w_in, conv_qkv_w, gdn_A_log, gdn_dt_bias, gdn_out_norm, lambda_q1, lambda_k1, lambda_q2, lambda_k2, diff_subln, w_branch_gdn, w_branch_diff, w_out, norm_mix_post, norm_ffn_pre, w_up, ffn_conv_w, ffn_conv_b, w_down, norm_ffn_post):
    depth = w_in.shape[0]
    for l in range(depth):
        lambda_init = 0.8 - 0.6 * math.exp(-0.3 * l)
        x = _layer(x, lambda_init, norm_mix_pre[l], w_in[l], conv_qkv_w[l], gdn_A_log[l],
                   gdn_dt_bias[l], gdn_out_norm[l], lambda_q1[l], lambda_k1[l], lambda_q2[l],
                   lambda_k2[l], diff_subln[l], w_branch_gdn[l], w_branch_diff[l], w_out[l],
                   norm_mix_post[l], norm_ffn_pre[l], w_up[l], ffn_conv_w[l], ffn_conv_b[l],
                   w_down[l], norm_ffn_post[l])
    return x
```

```python
import functools
import math

import jax
import jax.numpy as jnp
from jax import lax
from jax.experimental import pallas as pl
from jax.experimental.pallas import tpu as pltpu

F32 = jnp.float32
BF16 = jnp.bfloat16

D_MODEL = 1024
GDN_HEADS = 8
GDN_DK = 128
GDN_DV = 128
GDN_CONV = 4
DIFF_HEADS = 8
DIFF_DH = 64
D_FF = 2816
FFN_CONV = 3
NORM_EPS = 1e-6

GDN_QK = GDN_HEADS * GDN_DK
GDN_V = GDN_HEADS * GDN_DV
DIFF_QK = DIFF_HEADS * 2 * DIFF_DH
DIFF_V = DIFF_HEADS * 2 * DIFF_DH

LANES = 128
GDN_CHUNK = 128
VMEM_LIMIT = 56 * 1024 * 1024


def _dot(a, b):
    return jnp.dot(a, b, preferred_element_type=F32)


def _dot_nt(a, b):
    return lax.dot_general(a, b, (((1,), (1,)), ((), ())), preferred_element_type=F32)


def _bdot(a, b):
    return lax.dot_general(a, b, (((2,), (1,)), ((0,), (0,))), preferred_element_type=F32)


def _sigmoid(x):
    return 1.0 / (1.0 + jnp.exp(-x))


def _silu(x):
    hx = 0.5 * x
    return hx * jnp.tanh(hx) + hx


def _softplus(x):
    return jnp.maximum(x, 0.0) + jnp.log1p(jnp.exp(-jnp.abs(x)))


def _const_spec(shape):
    nd = len(shape)
    return pl.BlockSpec(shape, lambda *_: (0,) * nd, pipeline_mode=pl.Buffered(1))


_HALO = 8


def _causal_conv(u, prev, w, taps):
    tm, cw = u.shape
    row = lax.broadcasted_iota(jnp.int32, prev.shape, 0)
    u3 = u.reshape(tm // _HALO, _HALO, cw)
    y = u3 * w[taps - 1:taps, :]
    for sh in range(1, taps):
        r = pltpu.roll(u3, sh, 1)
        before = jnp.concatenate([pltpu.roll(prev, sh, 0)[None], r[:-1]], axis=0)
        y = y + jnp.where(row < sh, before, r) * w[taps - 1 - sh:taps - sh, :]
    return y.reshape(tm, cw)


_W_BLOCK = 512


def _cast_gdn_kernel(wt_ref, abt_ref, wg_ref, wab_ref):
    wg_ref[...] = wt_ref[...].T.astype(BF16)
    row = lax.broadcasted_iota(jnp.int32, abt_ref.shape, 0)
    wab_ref[...] = jnp.where(row < 2 * GDN_HEADS, abt_ref[...], 0.0).T.astype(BF16)


def _cast_diff_kernel(a_ref, b_ref, o_ref):
    off = 2 * GDN_HEADS
    x = jnp.concatenate([a_ref[...], b_ref[...]], axis=0)
    o_ref[...] = x[off:off + _W_BLOCK, :].T.astype(BF16)


def _cast_in_weights(w_in):
    n_gdn = 4 * GDN_QK
    n_diff = 3 * DIFF_QK + 2 * D_MODEL
    off = 2 * GDN_HEADS
    assert w_in.shape == (D_MODEL, n_gdn + off + n_diff)
    wt = jnp.transpose(w_in)
    rows = lambda height, f: pl.BlockSpec((height, D_MODEL), f)
    col = lambda width, f: pl.BlockSpec((D_MODEL, width), f)
    w_gdn, w_ab = pl.pallas_call(
        _cast_gdn_kernel,
        grid=(n_gdn // _W_BLOCK,),
        in_specs=[rows(_W_BLOCK, lambda j: (j, 0)), rows(LANES, lambda j: (n_gdn // LANES, 0))],
        out_specs=[col(_W_BLOCK, lambda j: (0, j)), col(LANES, lambda j: (0, 0))],
        out_shape=[jax.ShapeDtypeStruct((D_MODEL, n_gdn), BF16),
                   jax.ShapeDtypeStruct((D_MODEL, LANES), BF16)],
        compiler_params=pltpu.CompilerParams(dimension_semantics=("arbitrary",)),
        name="cast_w_gdn",
    )(wt, wt)
    w_diff = pl.pallas_call(
        _cast_diff_kernel,
        grid=(n_diff // _W_BLOCK,),
        in_specs=[rows(_W_BLOCK, lambda j: (n_gdn // _W_BLOCK + j, 0)),
                  rows(off, lambda j: ((n_gdn + _W_BLOCK) // off + (_W_BLOCK // off) * j, 0))],
        out_specs=col(_W_BLOCK, lambda j: (0, j)),
        out_shape=jax.ShapeDtypeStruct((D_MODEL, n_diff), BF16),
        compiler_params=pltpu.CompilerParams(dimension_semantics=("arbitrary",)),
        name="cast_w_diff",
    )(wt, wt)
    return w_gdn, w_ab, w_diff


_IN_CHUNK = 512


def _inproj_kernel(x_ref, g_ref, wg_ref, wab_ref, wd_ref, cw_ref, gp_ref,
                   qkv_ref, z_ref, gb_ref, dqkv_ref, gates_ref, h_scr, carry, *, tiles_per_seq):
    tm = x_ref.shape[0]

    @pl.when(pl.program_id(0) % tiles_per_seq == 0)
    def _():
        carry[...] = jnp.zeros_like(carry)

    x = x_ref[...]
    ms = jnp.mean(x * x, axis=-1, keepdims=True)
    h_scr[...] = (x * lax.rsqrt(ms + NORM_EPS) * g_ref[...]).astype(BF16)

    plain = ([(dqkv_ref, c0, c0) for c0 in range(0, 3 * DIFF_QK, _IN_CHUNK)]
             + [(gates_ref, c0, 3 * DIFF_QK + c0) for c0 in range(0, 2 * D_MODEL, _IN_CHUNK)])

    def emit_plain(count):
        for _ in range(min(count, len(plain))):
            out_ref, c0, w_off = plain.pop(0)
            acc = _dot(h_scr[...], wd_ref[:, w_off:w_off + _IN_CHUNK])
            out_ref[:, c0:c0 + _IN_CHUNK] = acc.astype(out_ref.dtype)

    for c0 in range(0, 3 * GDN_QK, _IN_CHUNK):
        emit_plain(1)
        u = _dot(h_scr[...], wg_ref[:, c0:c0 + _IN_CHUNK])
        prev = carry[:, c0:c0 + _IN_CHUNK]
        carry[:, c0:c0 + _IN_CHUNK] = u[tm - _HALO:tm, :]
        y = _silu(_causal_conv(u, prev, cw_ref[:, c0:c0 + _IN_CHUNK], GDN_CONV))
        if c0 < 2 * GDN_QK:
            scale = GDN_DK ** -0.5 if c0 < GDN_QK else 1.0
            heads = []
            for h0 in range(0, _IN_CHUNK, GDN_DK):
                yh = y[:, h0:h0 + GDN_DK]
                inv = lax.rsqrt(jnp.sum(yh * yh, axis=-1, keepdims=True) + NORM_EPS)
                heads.append(yh * (inv * scale))
            y = jnp.concatenate(heads, axis=1)
        qkv_ref[:, c0:c0 + _IN_CHUNK] = y.astype(BF16)

    for c0 in range(0, GDN_V, _IN_CHUNK):
        emit_plain(1)
        zc = _dot(h_scr[...], wg_ref[:, 3 * GDN_QK + c0:3 * GDN_QK + c0 + _IN_CHUNK])
        z_ref[:, c0:c0 + _IN_CHUNK] = _silu(zc).astype(BF16)

    ab = _dot(h_scr[...], wab_ref[...])
    gp = gp_ref[...]
    g = -jnp.exp(gp[0:1, :]) * _softplus(ab + gp[1:2, :])
    lane = lax.broadcasted_iota(jnp.int32, ab.shape, 1)
    gb_ref[...] = jnp.where(lane < GDN_HEADS, g, _sigmoid(ab))
    emit_plain(len(plain))


def _inproj(x2, g, w_gdn, w_ab, w_diff, conv_w, gparams, tm, tiles_per_seq):
    T = x2.shape[0]
    rows = lambda w: pl.BlockSpec((tm, w), lambda i: (i, 0))
    return pl.pallas_call(
        functools.partial(_inproj_kernel, tiles_per_seq=tiles_per_seq),
        grid=(T // tm,),
        in_specs=[
            rows(D_MODEL),
            _const_spec((1, D_MODEL)),
            _const_spec(w_gdn.shape),
            _const_spec(w_ab.shape),
            _const_spec(w_diff.shape),
            _const_spec((GDN_CONV, 3 * GDN_QK)),
            _const_spec((8, LANES)),
        ],
        out_specs=[rows(3 * GDN_QK), rows(GDN_V), rows(LANES), rows(3 * DIFF_QK), rows(2 * D_MODEL)],
        out_shape=[
            jax.ShapeDtypeStruct((T, 3 * GDN_QK), BF16),
            jax.ShapeDtypeStruct((T, GDN_V), BF16),
            jax.ShapeDtypeStruct((T, LANES), F32),
            jax.ShapeDtypeStruct((T, 3 * DIFF_QK), BF16),
            jax.ShapeDtypeStruct((T, 2 * D_MODEL), BF16),
        ],
        scratch_shapes=[pltpu.VMEM((tm, D_MODEL), BF16), pltpu.VMEM((_HALO, 3 * GDN_QK), F32)],
        compiler_params=pltpu.CompilerParams(
            dimension_semantics=("arbitrary",), vmem_limit_bytes=VMEM_LIMIT),
        name="inproj",
    )(x2, g, w_gdn, w_ab, w_diff, conv_w, gparams)


def _split3_dot(x, sel):
    hi = x.astype(BF16)
    r1 = x - hi.astype(F32)
    mid = r1.astype(BF16)
    lo = (r1 - mid.astype(F32)).astype(BF16)
    return _dot(hi, sel) + _dot(mid, sel) + _dot(lo, sel)


def _pair(a, b):
    return jnp.concatenate([a, b], axis=1)


def _blockdiag2(x):
    c = x.shape[0]
    z = jnp.zeros((c, c), x.dtype)
    return jnp.concatenate([_pair(x[:, :c], z), _pair(z, x[:, c:])], axis=0)


def _gdn_stages(q_ref, k_ref, v_ref, z_ref, g_ref, beta_ref, onorm_ref, o_ref):
    S = q_ref.shape[1]
    C = GDN_CHUNK
    NC = S // C

    ii = lax.broadcasted_iota(jnp.int32, (C, C), 0)
    jj = lax.broadcasted_iota(jnp.int32, (C, C), 1)
    ii2 = _pair(ii, ii)
    jj2 = _pair(jj, jj)
    strict2 = ii2 > jj2
    incl2 = ii2 >= jj2
    eye2 = (ii2 == jj2).astype(F32)

    gc = _split3_dot(g_ref[0, 0], (ii <= jj).astype(BF16))
    beta = beta_ref[0, 0]

    tiles = []
    for c in range(NC):
        gc_row = gc[c:c + 1, :]
        rg = jnp.broadcast_to(gc_row, (C, C))
        cg = rg.T
        glast = cg[C - 1:C, :]
        kf = k_ref[0, c * C:(c + 1) * C, :].astype(F32)
        eg = jnp.exp(cg)
        tiles.append(dict(
            decay=jnp.exp(jnp.minimum(cg - rg, 0.0)),
            beta_row=beta[c:c + 1, :],
            k=k_ref[0, c * C:(c + 1) * C, :],
            q=q_ref[0, c * C:(c + 1) * C, :],
            k_eg=(kf * eg).astype(BF16),
            q_dec=q_ref[0, c * C:(c + 1) * C, :].astype(F32) * eg,
            kd_t=(kf.T * (jnp.exp(glast - gc_row) * beta[c:c + 1, :])).astype(BF16),
            gl=jnp.exp(glast),
        ))
        if c % 4 == 3:
            yield

    pairs = [(tiles[c0], tiles[c0 + 1]) for c0 in range(0, NC, 2)]
    b_mats, aqs = [], []
    for ta, tb in pairs:
        k2 = _pair(ta["k"], tb["k"])
        kq = _dot_nt(jnp.concatenate([k2, _pair(ta["q"], tb["q"])], axis=0), _blockdiag2(k2))
        decay2 = _pair(ta["decay"], tb["decay"])
        beta2 = jnp.broadcast_to(_pair(ta["beta_row"], tb["beta_row"]), (C, 2 * C))
        b_mats.append(jnp.where(strict2, kq[0:C, :] * decay2, 0.0) * beta2)
        aqs.append((jnp.where(incl2, kq[C:2 * C, :] * decay2, 0.0) * beta2).astype(BF16))
    yield

    ps = [eye2 - jnp.where((ii2 >> 1) == (jj2 >> 1), b, 0.0) for b in b_mats]
    lvl = 1
    while (1 << lvl) < C:
        in_parent = (ii2 >> (lvl + 1)) == (jj2 >> (lvl + 1))
        in_child = (ii2 >> lvl) == (jj2 >> lvl)
        e_mask = jnp.logical_and(in_parent, jnp.logical_not(in_child))
        pbs = [p.astype(BF16) for p in ps]
        pes = [_dot(pb, _blockdiag2(jnp.where(e_mask, b, 0.0).astype(BF16)))
               for pb, b in zip(pbs, b_mats)]
        yield
        ps = [p - _dot(pe.astype(BF16), _blockdiag2(pb)) for p, pe, pb in zip(ps, pes, pbs)]
        lvl += 1
        yield

    wus = []
    for pi, (p, (ta, tb)) in enumerate(zip(ps, pairs)):
        pb = p.astype(BF16)
        for half, t in enumerate((ta, tb)):
            c = 2 * pi + half
            vc = v_ref[0, c * C:(c + 1) * C, :]
            wus.append(_dot(pb[:, half * C:(half + 1) * C], _pair(t["k_eg"], vc)).astype(BF16))

    yield

    mns, qes, aus = [], [], []
    for c, (t, wu) in enumerate(zip(tiles, wus)):
        aq_c = aqs[c // 2][:, (c % 2) * C:(c % 2 + 1) * C]
        r2 = _dot(jnp.concatenate([t["kd_t"], aq_c], axis=0), wu)
        mns.append(r2[0:C, :])
        qes.append((t["q_dec"] - r2[C:2 * C, 0:GDN_DK]).astype(BF16))
        aus.append(r2[C:2 * C, GDN_DK:GDN_DK + GDN_DV])

    yield

    steps = []
    for pi, (ta, tb) in enumerate(pairs):
        mn_a, mn_b = mns[2 * pi], mns[2 * pi + 1]
        ga, gb = ta["gl"], tb["gl"]
        mb = mn_b[:, 0:GDN_DK].astype(BF16)
        x = _dot(mb, mn_a.astype(BF16))
        m2 = gb * mn_a[:, 0:GDN_DK] + ga * mn_b[:, 0:GDN_DK] - x[:, 0:GDN_DK]
        steps.append(dict(
            m=jnp.concatenate([m2.astype(BF16), mn_a[:, 0:GDN_DK].astype(BF16)], axis=0),
            n2=gb * mn_a[:, GDN_DK:] - x[:, GDN_DK:] + mn_b[:, GDN_DK:],
            na=mn_a[:, GDN_DK:], g2=ga * gb, ga=ga))

    yield

    def emit_out(c, sb):
        o = _dot(qes[c], sb) + aus[c]
        ms = jnp.mean(o * o, axis=-1, keepdims=True)
        o = o * lax.rsqrt(ms + NORM_EPS) * onorm_ref[...]
        rows = slice(c * C, (c + 1) * C)
        o_ref[0, rows, :] = (o * z_ref[0, rows, :].astype(F32)).astype(o_ref.dtype)

    state = jnp.zeros((GDN_DK, GDN_DV), F32)
    for pi, st in enumerate(steps):
        sb = state.astype(BF16)
        r = _dot(st["m"], sb)
        mid = state * st["ga"] - r[C:2 * C, :] + st["na"]
        emit_out(2 * pi, sb)
        emit_out(2 * pi + 1, mid.astype(BF16))
        state = state * st["g2"] - r[0:C, :] + st["n2"]
        yield


_ATT_BLOCK = 256
_LOG2E = math.log2(math.e)


def _diffattn_stages(lam_ref, subln_ref, q_ref, k_ref, v_ref, o_ref, s_scr, p_scr, vext_scr,
                     lambda_init):
    S = q_ref.shape[1]
    tq = min(_ATT_BLOCK, S)
    dv = 2 * DIFF_DH
    neg = jnp.finfo(F32).min

    lp = lam_ref[...]
    lam = (jnp.exp(jnp.sum(lp[0:1, :] * lp[1:2, :], axis=-1, keepdims=True))
           - jnp.exp(jnp.sum(lp[2:3, :] * lp[3:4, :], axis=-1, keepdims=True))
           + lambda_init)

    vext_scr[:, 0:dv] = v_ref[0]
    vext_scr[:, dv:2 * dv] = jnp.ones((S, dv), BF16)

    lane = lax.broadcasted_iota(jnp.int32, (tq, dv), 1)
    causal = (lax.broadcasted_iota(jnp.int32, (2 * tq, tq), 1)
              <= jnp.bitwise_and(lax.broadcasted_iota(jnp.int32, (2 * tq, tq), 0), tq - 1))

    for qb in reversed(range(S // tq)):
        r0 = qb * tq
        n = r0 + tq
        s_buf = s_scr.at[qb % 2]
        p_buf = p_scr.at[qb % 2]
        qs = (q_ref[0, r0:n, :].astype(F32) * (DIFF_DH ** -0.5 * _LOG2E)).astype(BF16)
        zero = jnp.zeros_like(qs)
        qq = jnp.concatenate([jnp.where(lane < DIFF_DH, qs, zero),
                              jnp.where(lane >= DIFF_DH, qs, zero)], axis=0)

        m_run = None
        for c0 in range(0, n, tq):
            s = _dot_nt(qq, k_ref[0, c0:c0 + tq, :])
            if c0 == r0:
                s = jnp.where(causal, s, neg)
            s_buf[c0 // tq] = s
            for t in range(0, tq, LANES):
                tile = s[:, t:t + LANES]
                m_run = tile if m_run is None else jnp.maximum(m_run, tile)
        m_b = jnp.broadcast_to(jnp.max(m_run, axis=-1, keepdims=True), (2 * tq, LANES))
        yield

        for c0 in range(0, n, tq):
            for t in range(0, tq, LANES):
                p_buf[:, c0 + t:c0 + t + LANES] = jnp.exp2(
                    s_buf[c0 // tq, :, t:t + LANES] - m_b).astype(BF16)
        yield

        oe = _dot(p_buf[:, 0:n], vext_scr[0:n, :])
        a = (oe[0:tq, 0:dv] / oe[0:tq, dv:2 * dv]
             - lam * (oe[tq:2 * tq, 0:dv] / oe[tq:2 * tq, dv:2 * dv]))
        ms = jnp.mean(a * a, axis=-1, keepdims=True)
        a = a * lax.rsqrt(ms + NORM_EPS) * subln_ref[...] * (1.0 - lambda_init)
        o_ref[0, r0:n, :] = a.astype(o_ref.dtype)
        yield


def _mixers_kernel(gq_ref, gk_ref, gv_ref, gz_ref, g_ref, beta_ref, onorm_ref,
                   lam_ref, subln_ref, dq_ref, dk_ref, dv_ref,
                   ogdn_ref, odiff_ref, s_scr, p_scr, vext_scr, *, lambda_init):
    attn = _diffattn_stages(lam_ref, subln_ref, dq_ref, dk_ref, dv_ref, odiff_ref,
                            s_scr, p_scr, vext_scr, lambda_init)
    gdn = _gdn_stages(gq_ref, gk_ref, gv_ref, gz_ref, g_ref, beta_ref, onorm_ref, ogdn_ref)
    live = {attn: 1, gdn: 2}
    while live:
        for st, count in list(live.items()):
            for _ in range(count):
                if st in live and next(st, _DONE) is _DONE:
                    del live[st]


_DONE = object()


def _mixers(qkv3, zs3, gbt, onorm, dqkv3, lam_params, subln, lambda_init):
    B, S, _ = qkv3.shape
    NC = S // GDN_CHUNK
    H = GDN_HEADS
    assert DIFF_HEADS == H
    tq = min(_ATT_BLOCK, S)
    dv = 2 * DIFF_DH
    head = lambda off: pl.BlockSpec((1, S, LANES), lambda b, h, off=off: (b, 0, off + h))
    gate = lambda off: pl.BlockSpec((1, 1, NC, GDN_CHUNK), lambda b, h, off=off: (b, off + h, 0, 0))
    return pl.pallas_call(
        functools.partial(_mixers_kernel, lambda_init=lambda_init),
        grid=(B, H),
        in_specs=[
            head(0), head(H), head(2 * H), head(0), gate(0), gate(H),
            pl.BlockSpec((1, LANES), lambda b, h: (0, 0)),
            pl.BlockSpec((8, DIFF_DH), lambda b, h: (0, 0)),
            pl.BlockSpec((1, dv), lambda b, h: (0, 0)),
            head(0), head(H), head(2 * H),
        ],
        out_specs=[head(0), head(0)],
        out_shape=[jax.ShapeDtypeStruct((B, S, GDN_V), BF16),
                   jax.ShapeDtypeStruct((B, S, DIFF_V), BF16)],
        scratch_shapes=[
            pltpu.VMEM((2, S // tq, 2 * tq, tq), F32),
            pltpu.VMEM((2, 2 * tq, S), BF16),
            pltpu.VMEM((S, 2 * dv), BF16),
        ],
        compiler_params=pltpu.CompilerParams(
            dimension_semantics=("arbitrary", "arbitrary"), vmem_limit_bytes=VMEM_LIMIT),
        name="mixers",
    )(qkv3, qkv3, qkv3, zs3, gbt, gbt, onorm, lam_params, subln, dqkv3, dqkv3, dqkv3)


def _merge_kernel(x_ref, og_ref, od_ref, gates_ref, wbg_ref, wbd_ref, wout_ref,
                  npost_ref, npre_ref, x1_ref, h2_ref):
    tm = x_ref.shape[0]
    half = tm // 2
    for r0 in (0, half):
        rs = slice(r0, r0 + half)
        mg = _dot(og_ref[rs, :], wbg_ref[...])
        md = _dot(od_ref[rs, :], wbd_ref[...])
        gg = _sigmoid(gates_ref[rs, 0:D_MODEL].astype(F32))
        gd = _sigmoid(gates_ref[rs, D_MODEL:2 * D_MODEL].astype(F32))
        merged = (gg * mg + gd * md).astype(BF16)
        m = _dot(merged, wout_ref[...])
        ms = jnp.mean(m * m, axis=-1, keepdims=True)
        x1 = x_ref[rs, :] + m * lax.rsqrt(ms + NORM_EPS) * npost_ref[...]
        x1_ref[rs, :] = x1
        ms1 = jnp.mean(x1 * x1, axis=-1, keepdims=True)
        h2_ref[rs, :] = (x1 * lax.rsqrt(ms1 + NORM_EPS) * npre_ref[...]).astype(BF16)


def _merge(x2, og, od, gates, wbg, wbd, wout, npost, npre, tm):
    T = x2.shape[0]
    row = lambda w: pl.BlockSpec((tm, w), lambda i: (i, 0))
    return pl.pallas_call(
        _merge_kernel,
        grid=(T // tm,),
        in_specs=[
            row(D_MODEL), row(GDN_V), row(DIFF_V), row(2 * D_MODEL),
            _const_spec((GDN_V, D_MODEL)), _const_spec((DIFF_V, D_MODEL)),
            _const_spec((D_MODEL, D_MODEL)),
            _const_spec((1, D_MODEL)), _const_spec((1, D_MODEL)),
        ],
        out_specs=[row(D_MODEL), row(D_MODEL)],
        out_shape=[jax.ShapeDtypeStruct((T, D_MODEL), F32),
                   jax.ShapeDtypeStruct((T, D_MODEL), BF16)],
        compiler_params=pltpu.CompilerParams(
            dimension_semantics=("arbitrary",), vmem_limit_bytes=VMEM_LIMIT),
        name="merge",
    )(x2, og, od, gates, wbg, wbd, wout, npost, npre)


_FFN_CHUNK = 512


def _ffn_kernel(x1_ref, h2_ref, wup_ref, cw_ref, cb_ref, wdown_ref, npost_ref, out_ref,
                carry, act_scr):
    si = pl.program_id(1)
    tm = h2_ref.shape[1]

    @pl.when(si == 0)
    def _():
        carry[...] = jnp.zeros_like(carry)

    h2 = h2_ref[0]

    def conv_half(col0, cw):
        u = _dot(h2, wup_ref[:, col0:col0 + cw])
        prev = carry[:, col0:col0 + cw]
        carry[:, col0:col0 + cw] = u[tm - _HALO:tm, :]
        return _causal_conv(u, prev, cw_ref[:, col0:col0 + cw], FFN_CONV) + cb_ref[:, col0:col0 + cw]

    for c0 in range(0, D_FF, _FFN_CHUNK):
        cw = min(_FFN_CHUNK, D_FF - c0)
        gate = conv_half(c0, cw)
        up = conv_half(D_FF + c0, cw)
        act_scr[:, c0:c0 + cw] = (_silu(gate) * up).astype(BF16)

    f = _dot(act_scr[...], wdown_ref[...])
    ms = jnp.mean(f * f, axis=-1, keepdims=True)
    out_ref[0] = x1_ref[0] + f * lax.rsqrt(ms + NORM_EPS) * npost_ref[...]


def _ffn(x1_3, h2_3, wup, cw, cb, wdown, npost, tm):
    B, S, _ = x1_3.shape
    seq = lambda: pl.BlockSpec((1, tm, D_MODEL), lambda b, s: (b, s, 0))
    return pl.pallas_call(
        _ffn_kernel,
        grid=(B, S // tm),
        in_specs=[
            seq(), seq(),
            _const_spec((D_MODEL, 2 * D_FF)),
            _const_spec((FFN_CONV, 2 * D_FF)),
            _const_spec((1, 2 * D_FF)),
            _const_spec((D_FF, D_MODEL)),
            _const_spec((1, D_MODEL)),
        ],
        out_specs=seq(),
        out_shape=jax.ShapeDtypeStruct((B, S, D_MODEL), F32),
        scratch_shapes=[
            pltpu.VMEM((_HALO, 2 * D_FF), F32),
            pltpu.VMEM((tm, D_FF), BF16),
        ],
        compiler_params=pltpu.CompilerParams(
            dimension_semantics=("arbitrary", "arbitrary"), vmem_limit_bytes=VMEM_LIMIT),
        name="convffn",
    )(x1_3, h2_3, wup, cw, cb, wdown, npost)


def _row_tile(n, want):
    t = min(want, n)
    while n % t:
        t //= 2
    return t


def _layer(x, lambda_init, norm_mix_pre, w_in, conv_qkv_w, gdn_A_log, gdn_dt_bias, gdn_out_norm,
           lambda_q1, lambda_k1, lambda_q2, lambda_k2, diff_subln, w_branch_gdn, w_branch_diff,
           w_out, norm_mix_post, norm_ffn_pre, w_up, ffn_conv_w, ffn_conv_b, w_down, norm_ffn_post):
    B, S, D = x.shape
    T = B * S
    x2 = x.reshape(T, D)
    row2 = lambda v: v.reshape(1, -1).astype(F32)

    w_gdn, w_ab, w_diff = _cast_in_weights(w_in)
    gparams = jnp.zeros((8, LANES), F32)
    gparams = gparams.at[0, :GDN_HEADS].set(gdn_A_log.astype(F32))
    gparams = gparams.at[1, :GDN_HEADS].set(gdn_dt_bias.astype(F32))

    tm = _row_tile(S, 512)
    qkv, zs, gb, dqkv, gates = _inproj(x2, row2(norm_mix_pre), w_gdn, w_ab, w_diff,
                                       conv_qkv_w.astype(F32), gparams, tm, S // tm)

    gbt = jnp.transpose(gb.reshape(B, S, LANES)[:, :, :2 * GDN_HEADS], (0, 2, 1))
    gbt = gbt.reshape(B, 2 * GDN_HEADS, S // GDN_CHUNK, GDN_CHUNK)
    lam_params = jnp.zeros((8, DIFF_DH), F32)
    lam_params = lam_params.at[0].set(lambda_q1.astype(F32)).at[1].set(lambda_k1.astype(F32))
    lam_params = lam_params.at[2].set(lambda_q2.astype(F32)).at[3].set(lambda_k2.astype(F32))
    o_gdn, o_diff = _mixers(qkv.reshape(B, S, -1), zs.reshape(B, S, -1), gbt, row2(gdn_out_norm),
                            dqkv.reshape(B, S, -1), lam_params, row2(diff_subln), lambda_init)

    x1, h2 = _merge(x2, o_gdn.reshape(T, -1), o_diff.reshape(T, -1), gates,
                    w_branch_gdn.astype(BF16), w_branch_diff.astype(BF16), w_out.astype(BF16),
                    row2(norm_mix_post), row2(norm_ffn_pre), _row_tile(T, 1024))

    out = _ffn(x1.reshape(B, S, D), h2.reshape(B, S, D), w_up.astype(BF16),
               ffn_conv_w.astype(F32), row2(ffn_conv_b), w_down.astype(BF16),
               row2(norm_ffn_post), _row_tile(S, 512))
    return out


def kernel(x, norm_mix_pre, w_in, conv_qkv_w, gdn_A_log, gdn_dt_bias, gdn_out_norm, lambda_q1, lambda_k1, lambda_q2, lambda_k2, diff_subln, w_branch_gdn, w_branch_diff, w_out, norm_mix_post, norm_ffn_pre, w_up, ffn_conv_w, ffn_conv_b, w_down, norm_ffn_post):
    depth = w_in.shape[0]
    for l in range(depth):
        lambda_init = 0.8 - 0.6 * math.exp(-0.3 * l)
        x = _layer(x, lambda_init, norm_mix_pre[l], w_in[l], conv_qkv_w[l], gdn_A_log[l],
                   gdn_dt_bias[l], gdn_out_norm[l], lambda_q1[l], lambda_k1[l], lambda_q2[l],
                   lambda_k2[l], diff_subln[l], w_branch_gdn[l], w_branch_diff[l], w_out[l],
                   norm_mix_post[l], norm_ffn_pre[l], w_up[l], ffn_conv_w[l], ffn_conv_b[l],
                   w_down[l], norm_ffn_post[l])
    return x
```
